```python
import jax, jax.numpy as jnp
from jax import lax
import numpy as np

D_MODEL = 2048
BATCH = 2
SEQ = 4096
DEPTH = 2
DEC_BATCH = 128
DEC_SEQ = 4
PAST_LEN = 16384
PAGE_SIZE = 128

N_A = DEPTH // 2
N_B = DEPTH - N_A
A_HEADS = 8
A_DK = D_MODEL // (2 * A_HEADS)
A_DV = D_MODEL // A_HEADS
A_IN = 2 * A_HEADS * A_DK + 2 * A_HEADS * A_DV + 2 * A_HEADS
GATE_CAP = 15.0
MLSTM_CHUNK = 64
B_HEADS = 16
KV_RANK = 512
Q_RANK = 512
NOPE_DIM = 128
ROPE_DIM = 64
V_DIM = 128
ROPE_THETA = 10000.0
ATTN_SCALE = (NOPE_DIM + ROPE_DIM) ** -0.5
Q_BLOCK = 128
D_FF = ((8 * D_MODEL // 3 + 255) // 256) * 256
EPS = 1e-6

kernel_name = 'yoco_mlstm_mla_decode_step'


def rmsnorm(x, g):
    xf = x.astype(jnp.float32)
    y = xf * lax.rsqrt(jnp.mean(xf * xf, axis=-1, keepdims=True) + EPS)
    return (y * g.astype(jnp.float32)).astype(x.dtype)


def rope(x, pos):
    half = ROPE_DIM // 2
    freqs = ROPE_THETA ** (-jnp.arange(half, dtype=jnp.float32) / half)
    ang = pos.astype(jnp.float32)[:, None] * freqs[None, :]
    cos = jnp.cos(ang)[:, None, :]
    sin = jnp.sin(ang)[:, None, :]
    xf = x.astype(jnp.float32)
    x1, x2 = xf[..., :half], xf[..., half:]
    return jnp.concatenate([x1 * cos - x2 * sin, x1 * sin + x2 * cos], axis=-1).astype(x.dtype)


def softcap(z):
    return GATE_CAP * jnp.tanh(z / GATE_CAP)


def mlstm_scan(q, k, v, li, lf, C0, n0, m0, chunk):
    Bsz, H, T, _ = q.shape
    nc = T // chunk

    def chunks(a):
        return jnp.moveaxis(a.reshape(a.shape[:2] + (nc, chunk) + a.shape[3:]), 2, 0)

    causal = jnp.tril(jnp.ones((chunk, chunk), dtype=bool))

    def step(carry, inp):
        C, n, m_prev = carry
        qc, kc, vc, lic, lfc = inp
        b = jnp.cumsum(lfc, axis=-1)
        logw = jnp.where(causal, b[..., :, None] - b[..., None, :] + lic[..., None, :], -jnp.inf)
        log_inter = b + m_prev[..., None]
        m_t = jnp.maximum(log_inter, jnp.max(logw, axis=-1))
        s = jnp.einsum('bhtk,bhsk->bhts', qc, kc) * jnp.exp(logw - m_t[..., None])
        w_inter = jnp.exp(log_inter - m_t)
        num = jnp.einsum('bhts,bhsv->bhtv', s, vc) + w_inter[..., None] * jnp.einsum('bhtk,bhkv->bhtv', qc, C)
        den = jnp.sum(s, axis=-1) + w_inter * jnp.einsum('bhtk,bhk->bht', qc, n)
        h = num / jnp.maximum(jnp.abs(den), jnp.exp(-m_t))[..., None]
        m_end = m_t[..., -1]
        w_end = jnp.exp(b[..., -1:] - b + lic - m_end[..., None])
        decay = jnp.exp(b[..., -1] + m_prev - m_end)
        kw = kc * w_end[..., None]
        C_new = decay[..., None, None] * C + jnp.einsum('bhsk,bhsv->bhkv', kw, vc)
        n_new = decay[..., None] * n + jnp.sum(kw, axis=-2)
        return (C_new, n_new, m_end), h

    (C, n, m), h = lax.scan(step, (C0, n0, m0), (chunks(q), chunks(k), chunks(v), chunks(li), chunks(lf)))
    h = jnp.moveaxis(h, 0, 2).reshape(Bsz, H, T, -1)
    return h, C, n, m


def mlstm_mixer(h, w_in, b_gates, g_head, w_out, C0, n0, m0, chunk):
    Bsz, T, _ = h.shape
    hk, hv = A_HEADS * A_DK, A_HEADS * A_DV
    proj = h @ w_in
    f32 = jnp.float32

    def heads(a, d):
        return a.reshape(Bsz, T, A_HEADS, d).transpose(0, 2, 1, 3).astype(f32)

    q = heads(proj[..., :hk], A_DK)
    k = heads(proj[..., hk:2 * hk], A_DK) * (A_DK ** -0.5)
    v = heads(proj[..., 2 * hk:2 * hk + hv], A_DV)
    o = proj[..., 2 * hk + hv:2 * hk + 2 * hv]
    gates = (proj[..., 2 * hk + 2 * hv:] + b_gates).astype(f32).reshape(Bsz, T, 2, A_HEADS).transpose(2, 0, 3, 1)
    li = softcap(gates[0])
    lf = jax.nn.log_sigmoid(softcap(gates[1]))
    hc, C, n, m = mlstm_scan(q, k, v, li, lf, C0.astype(f32), n0.astype(f32), m0.astype(f32), chunk)
    hc = hc * lax.rsqrt(jnp.mean(hc * hc, axis=-1, keepdims=True) + EPS)
    hc = hc.transpose(0, 2, 1, 3).reshape(Bsz, T, hv) * g_head.astype(f32) * jax.nn.sigmoid(o.astype(f32))
    return hc.astype(h.dtype) @ w_out, C, n, m


def swiglu(h, w_in, w_out):
    gu = h @ w_in
    return (jax.nn.silu(gu[..., :D_FF]) * gu[..., D_FF:]) @ w_out


def shared_kv(x, pos, g_kv_in, w_kv_down, g_kv_latent):
    kv = rmsnorm(x, g_kv_in) @ w_kv_down
    ckv = rmsnorm(kv[..., :KV_RANK], g_kv_latent)
    krope = rope(kv[..., None, KV_RANK:], pos)[..., 0, :]
    return ckv, krope


def mla_prompt_attend(q_nope, q_rope, ckv, krope, w_uk, w_uv):
    Bsz, T = q_nope.shape[:2]
    k_nope = jnp.einsum('bsr,rhn->bshn', ckv, w_uk)
    v = jnp.einsum('bsr,rhv->bshv', ckv, w_uv)
    nb = T // Q_BLOCK

    def blocks(a):
        return jnp.moveaxis(a.reshape((Bsz, nb, Q_BLOCK) + a.shape[2:]), 1, 0)

    key_pos = jnp.arange(T)

    def one_block(args):
        qn, qr, start = args
        s = (jnp.einsum('bqhn,bshn->bhqs', qn, k_nope) + jnp.einsum('bqhp,bsp->bhqs', qr, krope)).astype(jnp.float32) * ATTN_SCALE
        q_pos = start + jnp.arange(Q_BLOCK)
        s = jnp.where(key_pos[None, :] <= q_pos[:, None], s, -jnp.inf)
        p = jax.nn.softmax(s, axis=-1).astype(v.dtype)
        return jnp.einsum('bhqs,bshv->bqhv', p, v)

    starts = jnp.arange(nb) * Q_BLOCK
    out = lax.map(one_block, (blocks(q_nope), blocks(q_rope), starts))
    return jnp.moveaxis(out, 0, 1).reshape(Bsz, T, B_HEADS, V_DIM)


def mla_sample_attend(q_nope, q_rope, ckv_new, krope_new, w_uk, w_uv, cache_ckv, cache_krope, page_table):
    T = q_nope.shape[1]
    past = page_table.shape[1] * cache_ckv.shape[1]
    q_lat = jnp.einsum('bthn,rhn->bthr', q_nope, w_uk)
    key_pos = jnp.arange(past + T)
    q_pos = past + jnp.arange(T)
    mask = key_pos[None, :] <= q_pos[:, None]

    def one_seq(args):
        ql, qr, cn, kn, pages = args
        ck = jnp.concatenate([cache_ckv[pages].reshape(past, KV_RANK), cn.astype(cache_ckv.dtype)], axis=0)
        kr = jnp.concatenate([cache_krope[pages].reshape(past, ROPE_DIM), kn.astype(cache_krope.dtype)], axis=0)
        s = (jnp.einsum('thr,sr->hts', ql, ck) + jnp.einsum('thp,sp->hts', qr, kr)).astype(jnp.float32) * ATTN_SCALE
        s = jnp.where(mask[None], s, -jnp.inf)
        p = jax.nn.softmax(s, axis=-1).astype(ck.dtype)
        return jnp.einsum('hts,sr->thr', p, ck)

    o_lat = lax.map(one_seq, (q_lat, q_rope, ckv_new, krope_new, page_table))
    return jnp.einsum('bthr,rhv->bthv', o_lat.astype(w_uv.dtype), w_uv)


def mla_mixer(h, pos, ckv, krope, attend, w_q_down, g_q_latent, w_q_up, w_uk, w_uv, w_out):
    Bsz, T, _ = h.shape
    q = (rmsnorm(h @ w_q_down, g_q_latent) @ w_q_up).reshape(Bsz, T, B_HEADS, NOPE_DIM + ROPE_DIM)
    q_nope, q_rope = q[..., :NOPE_DIM], rope(q[..., NOPE_DIM:], pos)
    o = attend(q_nope, q_rope, ckv, krope, w_uk, w_uv)
    return o.reshape(Bsz, T, B_HEADS * V_DIM) @ w_out


def trunk(x, pos, chunk, attend, C_in, n_in, m_in, g_mix, g_ffn, w_ffn_in, w_ffn_out,
          w_a_in, b_a_gates, g_a_head, w_a_out, g_kv_in, w_kv_down, g_kv_latent, w_uk, w_uv,
          w_q_down, g_q_latent, w_q_up, w_b_out, g_final):
    Cs, ns, ms = [], [], []
    ckv, krope = None, None
    for layer in range(DEPTH):
        if layer < N_A:
            a = layer
            y, C, n, m = mlstm_mixer(rmsnorm(x, g_mix[layer]), w_a_in[a], b_a_gates[a], g_a_head[a], w_a_out[a],
                                     C_in[a], n_in[a], m_in[a], chunk)
            x = x + y
            Cs.append(C)
            ns.append(n)
            ms.append(m)
        else:
            if layer == N_A:
                ckv, krope = shared_kv(x, pos, g_kv_in, w_kv_down, g_kv_latent)
            j = layer - N_A
            x = x + mla_mixer(rmsnorm(x, g_mix[layer]), pos, ckv, krope, attend,
                              w_q_down[j], g_q_latent[j], w_q_up[j], w_uk, w_uv, w_b_out[j])
        x = x + swiglu(rmsnorm(x, g_ffn[layer]), w_ffn_in[layer], w_ffn_out[layer])
    return rmsnorm(x, g_final), jnp.stack(Cs), jnp.stack(ns), jnp.stack(ms), ckv, krope


def setup_inputs(seed: int = 0) -> dict:
    key = jax.random.key(seed)
    ks = jax.random.split(key, 32)
    f32 = jnp.float32

    def nrm(k, shape, scale):
        return jax.random.normal(k, shape, f32) * scale

    n_pages = PAST_LEN // PAGE_SIZE
    n_pool = DEC_BATCH * n_pages * 5 // 4
    page_table = jax.random.permutation(ks[0], n_pool)[:DEC_BATCH * n_pages].reshape(DEC_BATCH, n_pages).astype(jnp.int32)
    return {
        'x_prompt': nrm(ks[1], (BATCH, SEQ, D_MODEL), 1.0),
        'x_sample': nrm(ks[2], (DEC_BATCH, DEC_SEQ, D_MODEL), 1.0),
        'cache_ckv': nrm(ks[3], (n_pool, PAGE_SIZE, KV_RANK), 1.0),
        'cache_krope': nrm(ks[4], (n_pool, PAGE_SIZE, ROPE_DIM), 1.0),
        'page_table': page_table,
        'state_C': nrm(ks[5], (N_A, DEC_BATCH, A_HEADS, A_DK, A_DV), A_DK ** -0.5),
        'state_n': nrm(ks[6], (N_A, DEC_BATCH, A_HEADS, A_DK), A_DK ** -0.5),
        'state_m': nrm(ks[7], (N_A, DEC_BATCH, A_HEADS), 1.0),
        'g_mix': 1.0 + nrm(ks[8], (DEPTH, D_MODEL), 0.02),
        'g_ffn': 1.0 + nrm(ks[9], (DEPTH, D_MODEL), 0.02),
        'w_ffn_in': nrm(ks[10], (DEPTH, D_MODEL, 2 * D_FF), D_MODEL ** -0.5),
        'w_ffn_out': nrm(ks[11], (DEPTH, D_FF, D_MODEL), D_FF ** -0.5),
        'w_a_in': nrm(ks[12], (N_A, D_MODEL, A_IN), D_MODEL ** -0.5),
        'b_a_gates': jnp.concatenate([nrm(ks[13], (N_A, A_HEADS), 0.1),
                                      3.0 + nrm(ks[14], (N_A, A_HEADS), 0.5)], axis=-1),
        'g_a_head': 1.0 + nrm(ks[15], (N_A, A_HEADS * A_DV), 0.02),
        'w_a_out': nrm(ks[16], (N_A, A_HEADS * A_DV, D_MODEL), (A_HEADS * A_DV) ** -0.5),
        'g_kv_in': 1.0 + nrm(ks[17], (D_MODEL,), 0.02),
        'w_kv_down': nrm(ks[18], (D_MODEL, KV_RANK + ROPE_DIM), D_MODEL ** -0.5),
        'g_kv_latent': 1.0 + nrm(ks[19], (KV_RANK,), 0.02),
        'w_uk': nrm(ks[20], (KV_RANK, B_HEADS, NOPE_DIM), KV_RANK ** -0.5),
        'w_uv': nrm(ks[21], (KV_RANK, B_HEADS, V_DIM), KV_RANK ** -0.5),
        'w_q_down': nrm(ks[22], (N_B, D_MODEL, Q_RANK), D_MODEL ** -0.5),
        'g_q_latent': 1.0 + nrm(ks[23], (N_B, Q_RANK), 0.02),
        'w_q_up': nrm(ks[24], (N_B, Q_RANK, B_HEADS * (NOPE_DIM + ROPE_DIM)), Q_RANK ** -0.5),
        'w_b_out': nrm(ks[25], (N_B, B_HEADS * V_DIM, D_MODEL), (B_HEADS * V_DIM) ** -0.5),
        'g_final': 1.0 + nrm(ks[26], (D_MODEL,), 0.02),
    }


def reference(x_prompt, x_sample, cache_ckv, cache_krope, page_table, state_C, state_n, state_m,
              g_mix, g_ffn, w_ffn_in, w_ffn_out, w_a_in, b_a_gates, g_a_head, w_a_out,
              g_kv_in, w_kv_down, g_kv_latent, w_uk, w_uv, w_q_down, g_q_latent, w_q_up, w_b_out, g_final):
    weights = (g_mix, g_ffn, w_ffn_in, w_ffn_out, w_a_in, b_a_gates, g_a_head, w_a_out,
               g_kv_in, w_kv_down, g_kv_latent, w_uk, w_uv, w_q_down, g_q_latent, w_q_up, w_b_out, g_final)
    Bp, Tp, _ = x_prompt.shape
    C0 = jnp.zeros((N_A, Bp, A_HEADS, A_DK, A_DV), jnp.float32)
    n0 = jnp.zeros((N_A, Bp, A_HEADS, A_DK), jnp.float32)
    m0 = jnp.zeros((N_A, Bp, A_HEADS), jnp.float32)
    y_prompt, C_p, n_p, m_p, ckv_p, krope_p = trunk(
        x_prompt, jnp.arange(Tp), min(MLSTM_CHUNK, Tp), mla_prompt_attend, C0, n0, m0, *weights)
    past = page_table.shape[1] * cache_ckv.shape[1]
    Ts = x_sample.shape[1]

    def attend_sample(qn, qr, ck, kr, wuk, wuv):
        return mla_sample_attend(qn, qr, ck, kr, wuk, wuv, cache_ckv, cache_krope, page_table)

    y_sample, C_s, n_s, m_s, ckv_s, krope_s = trunk(
        x_sample, past + jnp.arange(Ts), Ts, attend_sample, state_C, state_n, state_m, *weights)
    return (y_prompt, y_sample, C_p, n_p, m_p, ckv_p, krope_p, C_s, n_s, m_s, ckv_s, krope_s)
```

```python
import functools

import jax
import jax.numpy as jnp
from jax import lax
from jax.experimental import pallas as pl
from jax.experimental.pallas import tpu as pltpu

F32 = jnp.float32
BF16 = jnp.bfloat16

EPS = 1e-6
GATE_CAP = 15.0
ROPE_THETA = 10000.0

LANES_V7X = 128
BF16_SUBLANES_V7X = 16
VMEM_BYTES_V7X = 64 * 1024 * 1024
VMEM_BUDGET = VMEM_BYTES_V7X * 5 // 8
VMEM_LIMIT = VMEM_BYTES_V7X * 7 // 8

NT = (((1,), (1,)), ((), ()))
TN = (((0,), (0,)), ((), ()))


def _params(*sem):
    return pltpu.CompilerParams(dimension_semantics=sem, vmem_limit_bytes=VMEM_LIMIT)


def _largest_divisor(n, cap, mult):
    best = None
    for d in range(mult, min(n, cap) + 1, mult):
        if n % d == 0:
            best = d
    return n if best is None else best


def _row_tile(m, cap=1088):
    return _largest_divisor(m, cap, BF16_SUBLANES_V7X)


def _rms_kernel(x_ref, g_ref, *o_refs):
    x = x_ref[...]
    y = x * lax.rsqrt(jnp.mean(x * x, axis=-1, keepdims=True) + EPS)
    for i, o_ref in enumerate(o_refs):
        o_ref[...] = (y * g_ref[i:i + 1, :]).astype(o_ref.dtype)


def rmsnorm_rows(x, gains, out_dtype=BF16):
    m, d = x.shape
    n = gains.shape[0]
    tm = _row_tile(m, 512)
    outs = pl.pallas_call(
        _rms_kernel,
        grid=(m // tm,),
        in_specs=[pl.BlockSpec((tm, d), lambda i: (i, 0)),
                  pl.BlockSpec((n, d), lambda i: (0, 0))],
        out_specs=[pl.BlockSpec((tm, d), lambda i: (i, 0))] * n,
        out_shape=[jax.ShapeDtypeStruct((m, d), out_dtype)] * n,
        compiler_params=_params("arbitrary"),
        name="rmsnorm_rows",
    )(x, gains)
    return outs


def _mm_tiles(m, k, n, out_bytes, n_w=1, has_res=False):
    best = None
    for tn in (512, 256, 128):
        if n % tn:
            continue
        for tm in range(BF16_SUBLANES_V7X, min(m, 1088) + 1, BF16_SUBLANES_V7X):
            if m % tm:
                continue
            need = (2 * tm * k * 2 + n_w * (2 * k * tn * 4 + k * tn * 2)
                    + 2 * tm * tn * out_bytes + (2 * tm * tn * 4 if has_res else 0))
            if need <= VMEM_BUDGET and (best is None or tm * tn > best[0] * best[1]):
                best = (tm, tn)
    assert best is not None, (m, k, n)
    return best


def _mm_kernel(a_ref, w_ref, *rest, has_res):
    if has_res:
        r_ref, o_ref, w_sc = rest
    else:
        o_ref, w_sc = rest

    @pl.when(pl.program_id(1) == 0)
    def _():
        w_sc[...] = w_ref[...].astype(BF16)

    acc = jnp.dot(a_ref[...], w_sc[...], preferred_element_type=F32)
    if has_res:
        acc = acc + r_ref[...]
    o_ref[...] = acc.astype(o_ref.dtype)


def matmul(a, w, n_out=None, res=None, out_dtype=F32, name="matmul"):
    m, k = a.shape
    n = w.shape[1] if n_out is None else n_out
    tm, tn = _mm_tiles(m, k, n, jnp.dtype(out_dtype).itemsize, has_res=res is not None)
    in_specs = [pl.BlockSpec((tm, k), lambda j, i: (i, 0)),
                pl.BlockSpec((k, tn), lambda j, i: (0, j))]
    args = [a, w]
    if res is not None:
        in_specs.append(pl.BlockSpec((tm, tn), lambda j, i: (i, j)))
        args.append(res)
    return pl.pallas_call(
        functools.partial(_mm_kernel, has_res=res is not None),
        grid=(n // tn, m // tm),
        in_specs=in_specs,
        out_specs=pl.BlockSpec((tm, tn), lambda j, i: (i, j)),
        out_shape=jax.ShapeDtypeStruct((m, n), out_dtype),
        scratch_shapes=[pltpu.VMEM((k, tn), BF16)],
        compiler_params=_params("arbitrary", "arbitrary"),
        name=name,
    )(*args)


def _swiglu_kernel(a_ref, wg_ref, wu_ref, o_ref, wg_sc, wu_sc):
    @pl.when(pl.program_id(1) == 0)
    def _():
        wg_sc[...] = wg_ref[...].astype(BF16)
        wu_sc[...] = wu_ref[...].astype(BF16)

    a = a_ref[...]
    g = jnp.dot(a, wg_sc[...], preferred_element_type=F32)
    u = jnp.dot(a, wu_sc[...], preferred_element_type=F32)
    o_ref[...] = (g * jax.nn.sigmoid(g) * u).astype(o_ref.dtype)


def swiglu_in(a, w_in):
    m, k = a.shape
    f = w_in.shape[1] // 2
    tm, tf = _mm_tiles(m, k, f, 2, n_w=2)
    nf = f // tf
    return pl.pallas_call(
        _swiglu_kernel,
        grid=(nf, m // tm),
        in_specs=[pl.BlockSpec((tm, k), lambda j, i: (i, 0)),
                  pl.BlockSpec((k, tf), lambda j, i: (0, j)),
                  pl.BlockSpec((k, tf), lambda j, i: (0, j + nf))],
        out_specs=pl.BlockSpec((tm, tf), lambda j, i: (i, j)),
        out_shape=jax.ShapeDtypeStruct((m, f), BF16),
        scratch_shapes=[pltpu.VMEM((k, tf), BF16), pltpu.VMEM((k, tf), BF16)],
        compiler_params=_params("arbitrary", "arbitrary"),
        name="swiglu_in",
    )(a, w_in, w_in)


def _gates_kernel(wt_ref, a_ref, o_ref):
    o_ref[...] = lax.dot_general(wt_ref[...].astype(BF16), a_ref[...], NT,
                                 preferred_element_type=F32)


def gates_t(a, w_gate_t):
    m, k = a.shape
    g = w_gate_t.shape[0]
    tm = _largest_divisor(m, 512, LANES_V7X)
    return pl.pallas_call(
        _gates_kernel,
        grid=(m // tm,),
        in_specs=[pl.BlockSpec((g, k), lambda i: (0, 0)),
                  pl.BlockSpec((tm, k), lambda i: (i, 0))],
        out_specs=pl.BlockSpec((g, tm), lambda i: (0, i)),
        out_shape=jax.ShapeDtypeStruct((g, m), F32),
        compiler_params=_params("arbitrary"),
        name="gates_t",
    )(w_gate_t, a)


def _rope_lanes(x, a, b, c):
    w = x.shape[1]
    return x * a + pltpu.roll(x, 32, 1) * b + pltpu.roll(x, w - 32, 1) * c


def _qdown_kernel(a_ref, w_ref, g_ref, o_ref, w_sc):
    @pl.when(pl.program_id(0) == 0)
    def _():
        w_sc[...] = w_ref[...].astype(BF16)

    y = jnp.dot(a_ref[...], w_sc[...], preferred_element_type=F32)
    y = y * lax.rsqrt(jnp.mean(y * y, axis=-1, keepdims=True) + EPS)
    o_ref[...] = (y * g_ref[...]).astype(o_ref.dtype)


def q_down(a, w, g):
    m, k = a.shape
    r = w.shape[1]
    tm = _row_tile(m)
    return pl.pallas_call(
        _qdown_kernel,
        grid=(m // tm,),
        in_specs=[pl.BlockSpec((tm, k), lambda i: (i, 0)),
                  pl.BlockSpec((k, r), lambda i: (0, 0)),
                  pl.BlockSpec((1, r), lambda i: (0, 0))],
        out_specs=pl.BlockSpec((tm, r), lambda i: (i, 0)),
        out_shape=jax.ShapeDtypeStruct((m, r), BF16),
        scratch_shapes=[pltpu.VMEM((k, r), BF16)],
        compiler_params=_params("arbitrary"),
        name="q_down",
    )(a, w, g)


def _qup_kernel(a_ref, w_ref, ta_ref, tb_ref, tc_ref, o_ref, w_sc, *, hw):
    @pl.when(pl.program_id(1) == 0)
    def _():
        w_sc[...] = w_ref[...].astype(BF16)

    y = jnp.dot(a_ref[...], w_sc[...], preferred_element_type=F32)
    ta, tb, tc = ta_ref[...], tb_ref[...], tc_ref[...]
    for h in range(y.shape[1] // hw):
        sl = slice(h * hw, (h + 1) * hw)
        o_ref[:, sl] = _rope_lanes(y[:, sl], ta, tb, tc).astype(o_ref.dtype)


def q_up(a, w_pad, tabs, hw):
    m, k = a.shape
    n = w_pad.shape[1]
    tm, tn = _mm_tiles(m, k, n, 2)
    tn = max(tn, hw)
    tspec = pl.BlockSpec((tm, hw), lambda j, i: (i, 0))
    return pl.pallas_call(
        functools.partial(_qup_kernel, hw=hw),
        grid=(n // tn, m // tm),
        in_specs=[pl.BlockSpec((tm, k), lambda j, i: (i, 0)),
                  pl.BlockSpec((k, tn), lambda j, i: (0, j)),
                  tspec, tspec, tspec],
        out_specs=pl.BlockSpec((tm, tn), lambda j, i: (i, j)),
        out_shape=jax.ShapeDtypeStruct((m, n), BF16),
        scratch_shapes=[pltpu.VMEM((k, tn), BF16)],
        compiler_params=_params("arbitrary", "arbitrary"),
        name="q_up",
    )(a, w_pad, *tabs)


def _kvdown_kernel(a_ref, w_ref, g_ref, ta_ref, tb_ref, tc_ref,
                   ckv_ref, ckvb_ref, kr_ref, krb_ref, w_sc, *, rank):
    @pl.when(pl.program_id(0) == 0)
    def _():
        w_sc[...] = w_ref[...].astype(BF16)

    y = jnp.dot(a_ref[...], w_sc[...], preferred_element_type=F32)
    lat = y[:, :rank]
    lat = lat * lax.rsqrt(jnp.mean(lat * lat, axis=-1, keepdims=True) + EPS) * g_ref[...]
    ckv_ref[...] = lat
    ckvb_ref[...] = lat.astype(BF16)
    kr = _rope_lanes(y[:, rank:], ta_ref[...], tb_ref[...], tc_ref[...])
    kr_ref[...] = kr
    krb_ref[...] = kr.astype(BF16)


def kv_down(a, w_pad, g, tabs, rank):
    m, k = a.shape
    n = w_pad.shape[1]
    rw = n - rank
    tm = _row_tile(m)
    tspec = pl.BlockSpec((tm, rw), lambda i: (i, 1))
    return pl.pallas_call(
        functools.partial(_kvdown_kernel, rank=rank),
        grid=(m // tm,),
        in_specs=[pl.BlockSpec((tm, k), lambda i: (i, 0)),
                  pl.BlockSpec((k, n), lambda i: (0, 0)),
                  pl.BlockSpec((1, rank), lambda i: (0, 0)),
                  tspec, tspec, tspec],
        out_specs=[pl.BlockSpec((tm, rank), lambda i: (i, 0)),
                   pl.BlockSpec((tm, rank), lambda i: (i, 0)),
                   pl.BlockSpec((tm, rw), lambda i: (i, 0)),
                   pl.BlockSpec((tm, rw), lambda i: (i, 0))],
        out_shape=[jax.ShapeDtypeStruct((m, rank), F32),
                   jax.ShapeDtypeStruct((m, rank), BF16),
                   jax.ShapeDtypeStruct((m, rw), F32),
                   jax.ShapeDtypeStruct((m, rw), BF16)],
        scratch_shapes=[pltpu.VMEM((k, n), BF16)],
        compiler_params=_params("arbitrary"),
        name="kv_down",
    )(a, w_pad, g, *tabs)


def _split3(x):
    hi = x.astype(BF16)
    r = x - hi.astype(F32)
    mid = r.astype(BF16)
    lo = (r - mid.astype(F32)).astype(BF16)
    return hi, mid, lo


def _dot_exact(x, m01):
    return sum(jnp.dot(p, m01, preferred_element_type=F32) for p in _split3(x))


def _transpose_exact(eye, x):
    return sum(lax.dot_general(eye, p, NT, preferred_element_type=F32) for p in _split3(x))


def _log_sigmoid(x):
    return jnp.minimum(x, 0.0) - jnp.log1p(jnp.exp(-jnp.abs(x)))


def _softcap(z):
    return GATE_CAP * jnp.tanh(z / GATE_CAP)


def _mlstm_prompt_kernel(q_ref, k_ref, v_ref, o_ref, gt_ref, bias_ref, gh_ref,
                         hc_ref, c_out, n_out, m_out, c_sc, n_sc, m_sc, *, heads, dk, dv):
    c = pl.program_id(1)
    L = q_ref.shape[0]
    scale = dk ** -0.5

    @pl.when(c == 0)
    def _():
        c_sc[...] = jnp.zeros_like(c_sc)
        n_sc[...] = jnp.zeros_like(n_sc)
        m_sc[...] = jnp.zeros_like(m_sc)

    z = _softcap(gt_ref[...] + bias_ref[...])
    li_r = z[:heads]
    lf_r = _log_sigmoid(z[heads:])
    row = lax.broadcasted_iota(jnp.int32, (L, L), 0)
    col = lax.broadcasted_iota(jnp.int32, (L, L), 1)
    causal = col <= row
    upper = (row <= col).astype(BF16)
    eye = (row == col).astype(BF16)
    b_r = _dot_exact(lf_r, upper)
    cols = _transpose_exact(eye, jnp.concatenate([li_r, b_r], axis=0))

    for h in range(heads):
        q = q_ref[:, h * dk:(h + 1) * dk]
        k = k_ref[:, h * dk:(h + 1) * dk]
        v = v_ref[:, h * dv:(h + 1) * dv]
        li_c = cols[:, h:h + 1]
        b_c = cols[:, heads + h:heads + h + 1]
        m_prev = m_sc[h:h + 1, 0:1]
        n_prev = n_sc[h:h + 1, :]
        c_prev = c_sc[h]

        logw = jnp.where(causal, b_c - b_r[h:h + 1, :] + li_r[h:h + 1, :], -jnp.inf)
        log_inter = b_c + m_prev
        m_t = jnp.maximum(log_inter, jnp.max(logw, axis=1, keepdims=True))
        s = lax.dot_general(q, k, NT, preferred_element_type=F32) * scale * jnp.exp(logw - m_t)
        w_inter = jnp.exp(log_inter - m_t)
        num = (jnp.dot(s.astype(BF16), v, preferred_element_type=F32)
               + w_inter * jnp.dot(q, c_prev.astype(BF16), preferred_element_type=F32))
        qn = jnp.sum(q.astype(F32) * n_prev, axis=1, keepdims=True)
        den = jnp.sum(s, axis=1, keepdims=True) + w_inter * qn
        hh = num / jnp.maximum(jnp.abs(den), jnp.exp(-m_t))
        hh = hh * lax.rsqrt(jnp.mean(hh * hh, axis=1, keepdims=True) + EPS)
        og = o_ref[:, h * dv:(h + 1) * dv].astype(F32)
        hc_ref[:, h * dv:(h + 1) * dv] = (
            hh * gh_ref[:, h * dv:(h + 1) * dv] * jax.nn.sigmoid(og)).astype(hc_ref.dtype)

        m_end = m_t[L - 1:L, :]
        b_last = b_c[L - 1:L, :]
        w_end = jnp.exp(b_last - b_c + li_c - m_end)
        decay = jnp.exp(b_last + m_prev - m_end)
        kw = k.astype(F32) * (scale * w_end)
        c_sc[h] = decay * c_prev + lax.dot_general(kw.astype(BF16), v, TN,
                                                   preferred_element_type=F32)
        n_sc[h:h + 1, :] = decay * n_prev + jnp.sum(kw, axis=0, keepdims=True)
        m_sc[h:h + 1, :] = jnp.broadcast_to(m_end, (1, m_sc.shape[1]))

    @pl.when(c == pl.num_programs(1) - 1)
    def _():
        c_out[0] = c_sc[...]
        n_out[0] = n_sc[...]
        m_out[0] = m_sc[...]


def mlstm_prompt(proj, gates, bias, g_head, batch, seq, heads, dk, dv, chunk):
    hk, hv = heads * dk, heads * dv
    nc = seq // chunk
    row = lambda b, c: b * nc + c
    return pl.pallas_call(
        functools.partial(_mlstm_prompt_kernel, heads=heads, dk=dk, dv=dv),
        grid=(batch, nc),
        in_specs=[pl.BlockSpec((chunk, hk), lambda b, c: (row(b, c), 0)),
                  pl.BlockSpec((chunk, hk), lambda b, c: (row(b, c), 1)),
                  pl.BlockSpec((chunk, hv), lambda b, c: (row(b, c), 2 * hk // hv)),
                  pl.BlockSpec((chunk, hv), lambda b, c: (row(b, c), 2 * hk // hv + 1)),
                  pl.BlockSpec((2 * heads, chunk), lambda b, c: (0, row(b, c))),
                  pl.BlockSpec((2 * heads, 1), lambda b, c: (0, 0)),
                  pl.BlockSpec((1, hv), lambda b, c: (0, 0))],
        out_specs=[pl.BlockSpec((chunk, hv), lambda b, c: (row(b, c), 0)),
                   pl.BlockSpec((1, heads, dk, dv), lambda b, c: (b, 0, 0, 0)),
                   pl.BlockSpec((1, heads, dk), lambda b, c: (b, 0, 0)),
                   pl.BlockSpec((1, heads, LANES_V7X), lambda b, c: (b, 0, 0))],
        out_shape=[jax.ShapeDtypeStruct((batch * seq, hv), BF16),
                   jax.ShapeDtypeStruct((batch, heads, dk, dv), F32),
                   jax.ShapeDtypeStruct((batch, heads, dk), F32),
                   jax.ShapeDtypeStruct((batch, heads, LANES_V7X), F32)],
        scratch_shapes=[pltpu.VMEM((heads, dk, dv), F32),
                        pltpu.VMEM((heads, dk), F32),
                        pltpu.VMEM((heads, LANES_V7X), F32)],
        compiler_params=_params("arbitrary", "arbitrary"),
        name="mlstm_prompt",
    )(proj, proj, proj, proj, gates, bias, g_head)


def _mlstm_sample_kernel(q_ref, k_ref, v_ref, o_ref, gt_ref, bias_ref, gh_ref, m0_ref,
                         n0_ref, c0_ref, hc_ref, c_out, n_out, m_out, *, heads, dk, dv, ts):
    R = q_ref.shape[0]
    G = R // ts
    scale = dk ** -0.5

    z = _softcap(gt_ref[0] + bias_ref[...])
    li_r = z[:heads]
    lf_r = _log_sigmoid(z[heads:])
    row = lax.broadcasted_iota(jnp.int32, (R, R), 0)
    col = lax.broadcasted_iota(jnp.int32, (R, R), 1)
    seq_id = lambda i: sum((i >= g * ts).astype(jnp.int32) for g in range(1, G))
    same = seq_id(row) == seq_id(col)
    causal = same & (col <= row)
    upper = (same & (row <= col)).astype(BF16)
    eye = (row == col).astype(BF16)
    b_r = _dot_exact(lf_r, upper)
    cols = _transpose_exact(eye, jnp.concatenate([li_r, b_r], axis=0))
    seq_of_row = seq_id(lax.broadcasted_iota(jnp.int32, (R, 1), 0))
    m0 = m0_ref[0]

    for h in range(heads):
        q = q_ref[:, h * dk:(h + 1) * dk]
        k = k_ref[:, h * dk:(h + 1) * dk]
        v = v_ref[:, h * dv:(h + 1) * dv]
        li_c = cols[:, h:h + 1]
        b_c = cols[:, heads + h:heads + h + 1]
        m_prev = m0[:, h:h + 1]

        logw = jnp.where(causal, b_c - b_r[h:h + 1, :] + li_r[h:h + 1, :], -jnp.inf)
        log_inter = b_c + m_prev
        m_t = jnp.maximum(log_inter, jnp.max(logw, axis=1, keepdims=True))
        s = lax.dot_general(q, k, NT, preferred_element_type=F32) * scale * jnp.exp(logw - m_t)
        w_inter = jnp.exp(log_inter - m_t)

        qc = jnp.zeros((R, dv), F32)
        qn = jnp.zeros((R, 1), F32)
        qf = q.astype(F32)
        for g in range(G):
            mine = seq_of_row == g
            qc = jnp.where(mine, jnp.dot(q, c0_ref[g, h].astype(BF16),
                                         preferred_element_type=F32), qc)
            qn = jnp.where(mine, jnp.sum(qf * n0_ref[g, h:h + 1, :], axis=1, keepdims=True), qn)
        num = jnp.dot(s.astype(BF16), v, preferred_element_type=F32) + w_inter * qc
        den = jnp.sum(s, axis=1, keepdims=True) + w_inter * qn
        hh = num / jnp.maximum(jnp.abs(den), jnp.exp(-m_t))
        hh = hh * lax.rsqrt(jnp.mean(hh * hh, axis=1, keepdims=True) + EPS)
        og = o_ref[:, h * dv:(h + 1) * dv].astype(F32)
        hc_ref[:, h * dv:(h + 1) * dv] = (
            hh * gh_ref[:, h * dv:(h + 1) * dv] * jax.nn.sigmoid(og)).astype(hc_ref.dtype)

        kf = k.astype(F32)
        for g in range(G):
            last = (g + 1) * ts - 1
            m_end = m_t[last:last + 1, :]
            b_last = b_c[last:last + 1, :]
            mine = seq_of_row == g
            w_end = jnp.where(mine, jnp.exp(b_last - b_c + li_c - m_end), 0.0)
            decay = jnp.exp(b_last + m_prev[last:last + 1, :] - m_end)
            kw = kf * (scale * w_end)
            c_out[g, h] = decay * c0_ref[g, h] + lax.dot_general(
                kw.astype(BF16), v, TN, preferred_element_type=F32)
            n_out[g, h:h + 1, :] = decay * n0_ref[g, h:h + 1, :] + jnp.sum(kw, axis=0, keepdims=True)
            m_out[g, h:h + 1, :] = jnp.broadcast_to(m_end, (1, m_out.shape[2]))


def mlstm_sample(proj, gates_blk, bias, g_head, m0_blk, n0, c0, heads, dk, dv, ts, group):
    hk, hv = heads * dk, heads * dv
    rows = group * ts
    nb = c0.shape[0]
    steps = nb // group
    return pl.pallas_call(
        functools.partial(_mlstm_sample_kernel, heads=heads, dk=dk, dv=dv, ts=ts),
        grid=(steps,),
        in_specs=[pl.BlockSpec((rows, hk), lambda i: (i, 0)),
                  pl.BlockSpec((rows, hk), lambda i: (i, 1)),
                  pl.BlockSpec((rows, hv), lambda i: (i, 2 * hk // hv)),
                  pl.BlockSpec((rows, hv), lambda i: (i, 2 * hk // hv + 1)),
                  pl.BlockSpec((1, 2 * heads, rows), lambda i: (i, 0, 0)),
                  pl.BlockSpec((2 * heads, 1), lambda i: (0, 0)),
                  pl.BlockSpec((1, hv), lambda i: (0, 0)),
                  pl.BlockSpec((1, rows, heads), lambda i: (i, 0, 0)),
                  pl.BlockSpec((group, heads, dk), lambda i: (i, 0, 0)),
                  pl.BlockSpec((group, heads, dk, dv), lambda i: (i, 0, 0, 0))],
        out_specs=[pl.BlockSpec((rows, hv), lambda i: (i, 0)),
                   pl.BlockSpec((group, heads, dk, dv), lambda i: (i, 0, 0, 0)),
                   pl.BlockSpec((group, heads, dk), lambda i: (i, 0, 0)),
                   pl.BlockSpec((group, heads, LANES_V7X), lambda i: (i, 0, 0))],
        out_shape=[jax.ShapeDtypeStruct((nb * ts, hv), BF16),
                   jax.ShapeDtypeStruct((nb, heads, dk, dv), F32),
                   jax.ShapeDtypeStruct((nb, heads, dk), F32),
                   jax.ShapeDtypeStruct((nb, heads, LANES_V7X), F32)],
        compiler_params=_params("arbitrary"),
        name="mlstm_sample",
    )(proj, proj, proj, proj, gates_blk, bias, g_head, m0_blk, n0, c0)


def _softmax_step(s, pv_fn, m_sc, l_sc, acc_sc):
    m_prev = m_sc[...]
    m_new = jnp.maximum(m_prev, jnp.max(s, axis=1, keepdims=True))
    alpha = jnp.exp(m_prev - m_new)
    p = jnp.exp(s - m_new)
    l_sc[...] = alpha * l_sc[...] + jnp.sum(p, axis=1, keepdims=True)
    acc_sc[...] = alpha * acc_sc[...] + pv_fn(p.astype(BF16))
    m_sc[...] = m_new


def _attn_prompt_kernel(q_ref, kn_ref, kr_ref, v_ref, o_ref, kf_sc, m_sc, l_sc, acc_sc,
                        *, scale, nope):
    qi = pl.program_id(2)
    tq = q_ref.shape[0]

    @pl.when(qi == 0)
    def _():
        kf_sc[:, :nope] = kn_ref[...]
        kf_sc[:, nope:] = kr_ref[...]

    m_sc[...] = jnp.full_like(m_sc, -jnp.inf)
    l_sc[...] = jnp.zeros_like(l_sc)
    acc_sc[...] = jnp.zeros_like(acc_sc)
    q = q_ref[...]

    def block(ki, masked):
        start = pl.multiple_of(ki * tq, tq)
        s = lax.dot_general(q, kf_sc[pl.ds(start, tq), :], NT,
                            preferred_element_type=F32) * scale
        if masked:
            row = lax.broadcasted_iota(jnp.int32, s.shape, 0)
            col = lax.broadcasted_iota(jnp.int32, s.shape, 1)
            s = jnp.where(col <= row, s, -jnp.inf)
        _softmax_step(
            s, lambda p: jnp.dot(p, v_ref[pl.ds(start, tq), :], preferred_element_type=F32),
            m_sc, l_sc, acc_sc)

    def body(ki, carry):
        block(ki, False)
        return carry

    lax.fori_loop(0, qi, body, 0)
    block(qi, True)
    o_ref[...] = (acc_sc[...] / l_sc[...]).astype(o_ref.dtype)


def attn_prompt(q, kn, kr, v, batch, seq, heads, nope, vdim, scale, tq):
    hw = q.shape[1] // heads
    nq = seq // tq
    return pl.pallas_call(
        functools.partial(_attn_prompt_kernel, scale=scale, nope=nope),
        grid=(batch, heads, nq),
        in_specs=[pl.BlockSpec((tq, hw), lambda b, h, i: (b * nq + i, h)),
                  pl.BlockSpec((seq, nope), lambda b, h, i: (b, h)),
                  pl.BlockSpec((seq, hw - nope), lambda b, h, i: (b, 0)),
                  pl.BlockSpec((seq, vdim), lambda b, h, i: (b, h))],
        out_specs=pl.BlockSpec((tq, vdim), lambda b, h, i: (b * nq + i, h)),
        out_shape=jax.ShapeDtypeStruct((batch * seq, heads * vdim), BF16),
        scratch_shapes=[pltpu.VMEM((seq, hw), BF16),
                        pltpu.VMEM((tq, 1), F32),
                        pltpu.VMEM((tq, 1), F32),
                        pltpu.VMEM((tq, vdim), F32)],
        compiler_params=_params("arbitrary", "arbitrary", "arbitrary"),
        name="attn_prompt",
    )(q, kn, kr, v)


def _qlat_kernel(q_ref, w_ref, o_ref, *, nope):
    q = q_ref[...]
    lat = lax.dot_general(q[:, :nope], w_ref[...].astype(BF16), NT, preferred_element_type=F32)
    o_ref[0] = jnp.concatenate([lat.astype(BF16), q[:, nope:]], axis=1)


def q_latent(q, w_uk2, heads, nope):
    m = q.shape[0]
    hw = q.shape[1] // heads
    rank = w_uk2.shape[0]
    return pl.pallas_call(
        functools.partial(_qlat_kernel, nope=nope),
        grid=(heads,),
        in_specs=[pl.BlockSpec((m, hw), lambda h: (0, h)),
                  pl.BlockSpec((rank, nope), lambda h: (0, h))],
        out_specs=pl.BlockSpec((1, m, rank + hw - nope), lambda h: (h, 0, 0)),
        out_shape=jax.ShapeDtypeStruct((heads, m, rank + hw - nope), BF16),
        compiler_params=_params("arbitrary"),
        name="q_latent",
    )(q, w_uk2)


def _decode_kernel(pt_ref, q_ref, new_ref, *rest, pages, scale, rank, rope, ts):
    ck_refs = rest[:pages]
    kr_refs = rest[pages:2 * pages]
    o_ref, kc_sc, m_sc, l_sc, acc_sc = rest[2 * pages:]
    j = pl.program_id(1)
    psz = ck_refs[0].shape[0]
    width = kc_sc.shape[1]

    @pl.when(j == 0)
    def _():
        m_sc[...] = jnp.full_like(m_sc, -jnp.inf)
        l_sc[...] = jnp.zeros_like(l_sc)
        acc_sc[...] = jnp.zeros_like(acc_sc)
        kc_sc[:, rank + rope:] = jnp.zeros((kc_sc.shape[0], width - rank - rope), BF16)

    for p in range(pages):
        kc_sc[p * psz:(p + 1) * psz, :rank] = ck_refs[p][...].astype(BF16)
        kc_sc[p * psz:(p + 1) * psz, rank:rank + rope] = kr_refs[p][...].astype(BF16)

    q = q_ref[0]
    s = lax.dot_general(q, kc_sc[...], NT, preferred_element_type=F32) * scale
    _softmax_step(s, lambda p: jnp.dot(p, kc_sc[:, :rank], preferred_element_type=F32),
                  m_sc, l_sc, acc_sc)

    @pl.when(j == pl.num_programs(1) - 1)
    def _():
        new = new_ref[0]
        s2 = lax.dot_general(q, new, NT, preferred_element_type=F32) * scale
        qrow = lax.broadcasted_iota(jnp.int32, s2.shape, 0)
        heads = q.shape[0] // ts
        tok = sum((qrow >= t * heads).astype(jnp.int32) for t in range(1, ts))
        key = lax.broadcasted_iota(jnp.int32, s2.shape, 1)
        s2 = jnp.where(key <= tok, s2, -jnp.inf)
        _softmax_step(s2, lambda p: jnp.dot(p, new[:, :rank], preferred_element_type=F32),
                      m_sc, l_sc, acc_sc)
        o_ref[0] = acc_sc[...] / l_sc[...]


def attn_decode(q_full, new_keys, cache_ckv, cache_krope, page_table, scale, ts, pages):
    nb, rows, width = q_full.shape
    _, psz, rank = cache_ckv.shape
    rope = cache_krope.shape[2]
    n_pages = page_table.shape[1]
    nj = n_pages // pages

    def page_spec(p, last):
        return pl.BlockSpec((None, psz, last), lambda b, j, pt: (pt[b, j * pages + p], 0, 0))

    grid_spec = pltpu.PrefetchScalarGridSpec(
        num_scalar_prefetch=1,
        grid=(nb, nj),
        in_specs=([pl.BlockSpec((1, rows, width), lambda b, j, pt: (b, 0, 0)),
                   pl.BlockSpec((1, new_keys.shape[1], width), lambda b, j, pt: (b, 0, 0))]
                  + [page_spec(p, rank) for p in range(pages)]
                  + [page_spec(p, rope) for p in range(pages)]),
        out_specs=pl.BlockSpec((1, rows, rank), lambda b, j, pt: (b, 0, 0)),
        scratch_shapes=[pltpu.VMEM((pages * psz, width), BF16),
                        pltpu.VMEM((rows, 1), F32),
                        pltpu.VMEM((rows, 1), F32),
                        pltpu.VMEM((rows, rank), F32)],
    )
    return pl.pallas_call(
        functools.partial(_decode_kernel, pages=pages, scale=scale, rank=rank, rope=rope, ts=ts),
        grid_spec=grid_spec,
        out_shape=jax.ShapeDtypeStruct((nb, rows, rank), F32),
        compiler_params=_params("arbitrary", "arbitrary"),
        name="attn_decode",
    )(page_table, q_full, new_keys, *([cache_ckv] * pages), *([cache_krope] * pages))


def _oup_kernel(o_ref, w_ref, out_ref):
    out_ref[...] = jnp.dot(o_ref[0], w_ref[...].astype(BF16),
                           preferred_element_type=F32).astype(out_ref.dtype)


def o_up(o_lat, w_uv2, vdim):
    heads, m, rank = o_lat.shape
    return pl.pallas_call(
        _oup_kernel,
        grid=(heads,),
        in_specs=[pl.BlockSpec((1, m, rank), lambda h: (h, 0, 0)),
                  pl.BlockSpec((rank, vdim), lambda h: (0, h))],
        out_specs=pl.BlockSpec((m, vdim), lambda h: (0, h)),
        out_shape=jax.ShapeDtypeStruct((m, heads * vdim), BF16),
        compiler_params=_params("arbitrary"),
        name="o_up",
    )(o_lat, w_uv2)


def _rope_tables(pos, rope_dim, nope_pad, width):
    half = rope_dim // 2
    freqs = ROPE_THETA ** (-jnp.arange(half, dtype=F32) / half)
    ang = pos.astype(F32)[:, None] * freqs[None, :]
    cos, sin = jnp.cos(ang), jnp.sin(ang)
    n = pos.shape[0]
    zeros = lambda w: jnp.zeros((n, w), F32)
    tail = width - nope_pad - rope_dim
    ta = jnp.concatenate([jnp.ones((n, nope_pad), F32), cos, cos, zeros(tail)], axis=1)
    tb = jnp.concatenate([zeros(nope_pad + half), sin, zeros(tail)], axis=1)
    tc = jnp.concatenate([zeros(nope_pad), -sin, zeros(half + tail)], axis=1)
    return ta, tb, tc


def kernel(x_prompt, x_sample, cache_ckv, cache_krope, page_table, state_C, state_n, state_m,
           g_mix, g_ffn, w_ffn_in, w_ffn_out, w_a_in, b_a_gates, g_a_head, w_a_out,
           g_kv_in, w_kv_down, g_kv_latent, w_uk, w_uv, w_q_down, g_q_latent, w_q_up,
           w_b_out, g_final):
    bp, tp, d = x_prompt.shape
    bs, ts, _ = x_sample.shape
    mp, ms = bp * tp, bs * ts
    n_a = state_C.shape[0]
    n_layers = g_mix.shape[0]
    assert n_a == 1 and n_layers == 2, "one mLSTM layer followed by one MLA layer"
    _, _, heads_a, dk, dv = state_C.shape
    hk, hv = heads_a * dk, heads_a * dv
    rank, heads_b, nope = w_uk.shape
    vdim = w_uv.shape[2]
    rope = cache_krope.shape[2]
    psz = cache_ckv.shape[1]
    past = page_table.shape[1] * psz
    hw = nope + 2 * rope
    assert nope % LANES_V7X == 0 and hw % LANES_V7X == 0 and (2 * rope) % LANES_V7X == 0
    attn_scale = (nope + rope) ** -0.5

    x = jnp.concatenate([x_prompt.reshape(mp, d), x_sample.reshape(ms, d)], axis=0)
    pos = jnp.concatenate([jnp.tile(jnp.arange(tp), bp), jnp.tile(past + jnp.arange(ts), bs)])
    tabs = _rope_tables(pos, rope, nope, hw)

    (xn,) = rmsnorm_rows(x, g_mix[0:1])
    proj = matmul(xn, w_a_in[0], n_out=2 * hk + 2 * hv, out_dtype=BF16, name="mlstm_in_proj")
    gates = gates_t(xn, w_a_in[0][:, 2 * hk + 2 * hv:].T)
    bias = b_a_gates[0].reshape(2 * heads_a, 1)
    g_head = g_a_head[0].reshape(1, hv)

    hc_p, c_p, n_p, m_p = mlstm_prompt(proj, gates, bias, g_head, bp, tp, heads_a, dk, dv,
                                       chunk=min(256, tp))
    group = 4
    rows = group * ts
    proj_s = proj[mp:]
    gates_blk = gates[:, mp:].reshape(2 * heads_a, bs // group, rows).transpose(1, 0, 2)
    m0_blk = jnp.repeat(state_m[0], ts, axis=0).reshape(bs // group, rows, heads_a)
    hc_s, c_s, n_s, m_s = mlstm_sample(proj_s, gates_blk, bias, g_head, m0_blk, state_n[0],
                                       state_C[0], heads_a, dk, dv, ts, group)
    hc = jnp.concatenate([hc_p, hc_s], axis=0)
    x = matmul(hc, w_a_out[0], res=x, name="mlstm_out_proj")

    (xn,) = rmsnorm_rows(x, g_ffn[0:1])
    x = matmul(swiglu_in(xn, w_ffn_in[0]), w_ffn_out[0], res=x, name="ffn_out")

    xn_kv, xn_q = rmsnorm_rows(x, jnp.stack([g_kv_in, g_mix[1]]))
    w_kv_pad = jnp.pad(w_kv_down, ((0, 0), (0, rope)))
    ckv, ckv_b, kr, kr_b = kv_down(xn_kv, w_kv_pad, g_kv_latent.reshape(1, rank), tabs, rank)

    xq = q_down(xn_q, w_q_down[0], g_q_latent[0].reshape(1, rank))
    w_qup_pad = jnp.pad(w_q_up[0].reshape(rank, heads_b, nope + rope),
                        ((0, 0), (0, 0), (0, rope))).reshape(rank, heads_b * hw)
    q = q_up(xq, w_qup_pad, tabs, hw)

    w_uk2 = w_uk.reshape(rank, heads_b * nope)
    w_uv2 = w_uv.reshape(rank, heads_b * vdim)
    ckv_bp = ckv_b[:mp]
    kn = matmul(ckv_bp, w_uk2, out_dtype=BF16, name="k_up")
    v = matmul(ckv_bp, w_uv2, out_dtype=BF16, name="v_up")
    o_p = attn_prompt(q, kn, kr_b, v, bp, tp, heads_b, nope, vdim, attn_scale, tq=min(512, tp))

    q_lat = q_latent(q[mp:], w_uk2, heads_b, nope)
    width = rank + 2 * rope
    q_full = q_lat.reshape(heads_b, bs, ts, width).transpose(1, 2, 0, 3).reshape(bs, ts * heads_b, width)
    new_keys = jnp.concatenate([ckv_b[mp:], kr_b[mp:]], axis=1).reshape(bs, ts, width)
    new_keys = jnp.pad(new_keys, ((0, 0), (0, BF16_SUBLANES_V7X - ts), (0, 0)))
    o_lat = attn_decode(q_full, new_keys, cache_ckv, cache_krope, page_table, attn_scale, ts,
                        pages=min(16, page_table.shape[1]))
    o_lat = o_lat.reshape(bs, ts, heads_b, rank).transpose(2, 0, 1, 3).reshape(heads_b, ms, rank)
    o_s = o_up(o_lat.astype(BF16), w_uv2, vdim)

    x = matmul(jnp.concatenate([o_p, o_s], axis=0), w_b_out[0], res=x, name="mla_out_proj")

    (xn,) = rmsnorm_rows(x, g_ffn[1:2])
    x = matmul(swiglu_in(xn, w_ffn_in[1]), w_ffn_out[1], res=x, name="ffn_out")
    (y,) = rmsnorm_rows(x, g_final.reshape(1, d), out_dtype=F32)

    return (y[:mp].reshape(bp, tp, d), y[mp:].reshape(bs, ts, d),
            c_p[None], n_p[None], m_p[None, :, :, 0],
            ckv[:mp].reshape(bp, tp, rank), kr[:mp, :rope].reshape(bp, tp, rope),
            c_s[None], n_s[None], m_s[None, :, :, 0],
            ckv[mp:].reshape(bs, ts, rank), kr[mp:, :rope].reshape(bs, ts, rope))
```

```python
import functools

import jax
import jax.numpy as jnp
from jax import lax
from jax.experimental import pallas as pl
from jax.experimental.pallas import tpu as pltpu

F32 = jnp.float32
BF16 = jnp.bfloat16

EPS = 1e-6
GATE_CAP = 15.0
ROPE_THETA = 10000.0
LOG2E = 1.4426950408889634

LANES_V7X = 128
BF16_SUBLANES_V7X = 16
VMEM_BYTES_V7X = 64 * 1024 * 1024
VMEM_BUDGET = VMEM_BYTES_V7X * 5 // 8
VMEM_LIMIT = VMEM_BYTES_V7X * 7 // 8

NT = (((1,), (1,)), ((), ()))
TN = (((0,), (0,)), ((), ()))


def _params(*sem):
    return pltpu.CompilerParams(dimension_semantics=sem, vmem_limit_bytes=VMEM_LIMIT)


def _largest_divisor(n, cap, mult):
    best = None
    for d in range(mult, min(n, cap) + 1, mult):
        if n % d == 0:
            best = d
    return n if best is None else best


def _row_tile(m, cap=1088):
    return _largest_divisor(m, cap, BF16_SUBLANES_V7X)


def _rms_kernel(x_ref, g_ref, *o_refs):
    x = x_ref[...]
    y = x * lax.rsqrt(jnp.mean(x * x, axis=-1, keepdims=True) + EPS)
    for i, o_ref in enumerate(o_refs):
        o_ref[...] = (y * g_ref[i:i + 1, :]).astype(o_ref.dtype)


def rmsnorm_rows(x, gains, out_dtype=BF16, row_start=0, n_rows=None):
    d = x.shape[1]
    m = x.shape[0] if n_rows is None else n_rows
    n = gains.shape[0]
    tm = _largest_divisor(m, 512, BF16_SUBLANES_V7X)
    assert row_start % tm == 0
    first = row_start // tm
    outs = pl.pallas_call(
        _rms_kernel,
        grid=(m // tm,),
        in_specs=[pl.BlockSpec((tm, d), lambda i: (i + first, 0)),
                  pl.BlockSpec((n, d), lambda i: (0, 0))],
        out_specs=[pl.BlockSpec((tm, d), lambda i: (i, 0))] * n,
        out_shape=[jax.ShapeDtypeStruct((m, d), out_dtype)] * n,
        compiler_params=_params("arbitrary"),
        name="rmsnorm_rows",
    )(x, gains)
    return outs


def _mm_tiles(m, k, n, out_bytes, n_w=1, has_res=False):
    best = None
    for tn in (512, 256, 128):
        if n % tn:
            continue
        for tm in range(BF16_SUBLANES_V7X, min(m, 1088) + 1, BF16_SUBLANES_V7X):
            if m % tm:
                continue
            need = (2 * tm * k * 2 + n_w * (2 * k * tn * 4 + k * tn * 2)
                    + 2 * tm * tn * out_bytes + (2 * tm * tn * 4 if has_res else 0))
            if need <= VMEM_BUDGET and (best is None or tm * tn > best[0] * best[1]):
                best = (tm, tn)
    assert best is not None, (m, k, n)
    return best


def _mm_kernel(a_ref, w_ref, *rest, has_res, w_t):
    if has_res:
        r_ref, o_ref, w_sc = rest
    else:
        o_ref, w_sc = rest

    @pl.when(pl.program_id(1) == 0)
    def _():
        w_sc[...] = w_ref[...].astype(BF16)

    if w_t:
        acc = lax.dot_general(a_ref[...], w_sc[...], NT, preferred_element_type=F32)
    else:
        acc = jnp.dot(a_ref[...], w_sc[...], preferred_element_type=F32)
    if has_res:
        acc = acc + r_ref[...]
    o_ref[...] = acc.astype(o_ref.dtype)


def matmul(a, w, n_out=None, res=None, out_dtype=F32, w_t=False, name="matmul"):
    m, k = a.shape
    n = (w.shape[0] if w_t else w.shape[1]) if n_out is None else n_out
    tm, tn = _mm_tiles(m, k, n, jnp.dtype(out_dtype).itemsize, has_res=res is not None)
    w_block = (tn, k) if w_t else (k, tn)
    w_map = (lambda j, i: (j, 0)) if w_t else (lambda j, i: (0, j))
    in_specs = [pl.BlockSpec((tm, k), lambda j, i: (i, 0)),
                pl.BlockSpec(w_block, w_map)]
    args = [a, w]
    if res is not None:
        in_specs.append(pl.BlockSpec((tm, tn), lambda j, i: (i, j)))
        args.append(res)
    return pl.pallas_call(
        functools.partial(_mm_kernel, has_res=res is not None, w_t=w_t),
        grid=(n // tn, m // tm),
        in_specs=in_specs,
        out_specs=pl.BlockSpec((tm, tn), lambda j, i: (i, j)),
        out_shape=jax.ShapeDtypeStruct((m, n), out_dtype),
        scratch_shapes=[pltpu.VMEM(w_block, BF16)],
        compiler_params=_params("arbitrary", "arbitrary"),
        name=name,
    )(*args)


def _swiglu_kernel(a_ref, wg_ref, wu_ref, o_ref, wg_sc, wu_sc):
    @pl.when(pl.program_id(1) == 0)
    def _():
        wg_sc[...] = wg_ref[...].astype(BF16)
        wu_sc[...] = wu_ref[...].astype(BF16)

    a = a_ref[...]
    g = jnp.dot(a, wg_sc[...], preferred_element_type=F32)
    u = jnp.dot(a, wu_sc[...], preferred_element_type=F32)
    o_ref[...] = (g * jax.nn.sigmoid(g) * u).astype(o_ref.dtype)


def swiglu_in(a, w_in):
    m, k = a.shape
    f = w_in.shape[1] // 2
    tm, tf = _mm_tiles(m, k, f, 2, n_w=2)
    nf = f // tf
    return pl.pallas_call(
        _swiglu_kernel,
        grid=(nf, m // tm),
        in_specs=[pl.BlockSpec((tm, k), lambda j, i: (i, 0)),
                  pl.BlockSpec((k, tf), lambda j, i: (0, j)),
                  pl.BlockSpec((k, tf), lambda j, i: (0, j + nf))],
        out_specs=pl.BlockSpec((tm, tf), lambda j, i: (i, j)),
        out_shape=jax.ShapeDtypeStruct((m, f), BF16),
        scratch_shapes=[pltpu.VMEM((k, tf), BF16), pltpu.VMEM((k, tf), BF16)],
        compiler_params=_params("arbitrary", "arbitrary"),
        name="swiglu_in",
    )(a, w_in, w_in)


def _gates_kernel(wt_ref, a_ref, o_ref):
    o_ref[...] = lax.dot_general(wt_ref[...].astype(BF16), a_ref[...], NT,
                                 preferred_element_type=F32)


def gates_t(a, w_gate_t):
    m, k = a.shape
    g = w_gate_t.shape[0]
    tm = _largest_divisor(m, 512, LANES_V7X)
    return pl.pallas_call(
        _gates_kernel,
        grid=(m // tm,),
        in_specs=[pl.BlockSpec((g, k), lambda i: (0, 0)),
                  pl.BlockSpec((tm, k), lambda i: (i, 0))],
        out_specs=pl.BlockSpec((g, tm), lambda i: (0, i)),
        out_shape=jax.ShapeDtypeStruct((g, m), F32),
        compiler_params=_params("arbitrary"),
        name="gates_t",
    )(w_gate_t, a)


def _rope_lanes(x, a, b, c):
    w = x.shape[1]
    return x * a + pltpu.roll(x, 32, 1) * b + pltpu.roll(x, w - 32, 1) * c


def _qdown_kernel(a_ref, w_ref, g_ref, o_ref, w_sc):
    @pl.when(pl.program_id(0) == 0)
    def _():
        w_sc[...] = w_ref[...].astype(BF16)

    y = jnp.dot(a_ref[...], w_sc[...], preferred_element_type=F32)
    y = y * lax.rsqrt(jnp.mean(y * y, axis=-1, keepdims=True) + EPS)
    o_ref[...] = (y * g_ref[...]).astype(o_ref.dtype)


def q_down(a, w, g):
    m, k = a.shape
    r = w.shape[1]
    tm = _row_tile(m)
    return pl.pallas_call(
        _qdown_kernel,
        grid=(m // tm,),
        in_specs=[pl.BlockSpec((tm, k), lambda i: (i, 0)),
                  pl.BlockSpec((k, r), lambda i: (0, 0)),
                  pl.BlockSpec((1, r), lambda i: (0, 0))],
        out_specs=pl.BlockSpec((tm, r), lambda i: (i, 0)),
        out_shape=jax.ShapeDtypeStruct((m, r), BF16),
        scratch_shapes=[pltpu.VMEM((k, r), BF16)],
        compiler_params=_params("arbitrary"),
        name="q_down",
    )(a, w, g)


def _qup_kernel(a_ref, w_ref, ta_ref, tb_ref, tc_ref, o_ref, w_sc, *, hw):
    @pl.when(pl.program_id(0) == 0)
    def _():
        w_sc[...] = w_ref[...].astype(BF16)

    a = a_ref[...]
    ta, tb, tc = ta_ref[...], tb_ref[...], tc_ref[...]
    for h in range(w_sc.shape[1] // hw):
        sl = slice(h * hw, (h + 1) * hw)
        y = jnp.dot(a, w_sc[:, sl], preferred_element_type=F32)
        o_ref[:, sl] = _rope_lanes(y, ta, tb, tc).astype(o_ref.dtype)


def q_up(a, w_pad, tabs, hw):
    m, k = a.shape
    n = w_pad.shape[1]
    tm = _row_tile(m, 544)
    tspec = pl.BlockSpec((tm, hw), lambda i: (i, 0))
    return pl.pallas_call(
        functools.partial(_qup_kernel, hw=hw),
        grid=(m // tm,),
        in_specs=[pl.BlockSpec((tm, k), lambda i: (i, 0)),
                  pl.BlockSpec((k, n), lambda i: (0, 0)),
                  tspec, tspec, tspec],
        out_specs=pl.BlockSpec((tm, n), lambda i: (i, 0)),
        out_shape=jax.ShapeDtypeStruct((m, n), BF16),
        scratch_shapes=[pltpu.VMEM((k, n), BF16)],
        compiler_params=_params("arbitrary"),
        name="q_up",
    )(a, w_pad, *tabs)


def _kvdown_kernel(a_ref, w_ref, g_ref, ta_ref, tb_ref, tc_ref,
                   ckv_ref, ckvb_ref, kr_ref, krb_ref, w_sc, *, rank):
    @pl.when(pl.program_id(0) == 0)
    def _():
        w_sc[...] = w_ref[...].astype(BF16)

    y = jnp.dot(a_ref[...], w_sc[...], preferred_element_type=F32)
    lat = y[:, :rank]
    lat = lat * lax.rsqrt(jnp.mean(lat * lat, axis=-1, keepdims=True) + EPS) * g_ref[...]
    ckv_ref[...] = lat
    ckvb_ref[...] = lat.astype(BF16)
    kr = _rope_lanes(y[:, rank:], ta_ref[...], tb_ref[...], tc_ref[...])
    kr_ref[...] = kr[:, :kr_ref.shape[1]]
    krb_ref[...] = kr.astype(BF16)


def kv_down(a, w_pad, g, tabs, rank, rope, row_start, n_rows):
    k = a.shape[1]
    m = n_rows
    n = w_pad.shape[1]
    rw = n - rank
    tm = _row_tile(m)
    assert row_start % tm == 0
    first = row_start // tm
    tspec = pl.BlockSpec((tm, rw), lambda i: (i + first, 1))
    return pl.pallas_call(
        functools.partial(_kvdown_kernel, rank=rank),
        grid=(m // tm,),
        in_specs=[pl.BlockSpec((tm, k), lambda i: (i + first, 0)),
                  pl.BlockSpec((k, n), lambda i: (0, 0)),
                  pl.BlockSpec((1, rank), lambda i: (0, 0)),
                  tspec, tspec, tspec],
        out_specs=[pl.BlockSpec((tm, rank), lambda i: (i, 0)),
                   pl.BlockSpec((tm, rank), lambda i: (i, 0)),
                   pl.BlockSpec((tm, rope), lambda i: (i, 0)),
                   pl.BlockSpec((tm, rw), lambda i: (i, 0))],
        out_shape=[jax.ShapeDtypeStruct((m, rank), F32),
                   jax.ShapeDtypeStruct((m, rank), BF16),
                   jax.ShapeDtypeStruct((m, rope), F32),
                   jax.ShapeDtypeStruct((m, rw), BF16)],
        scratch_shapes=[pltpu.VMEM((k, n), BF16)],
        compiler_params=_params("arbitrary"),
        name="kv_down",
    )(a, w_pad, g, *tabs)


def _split3(x):
    hi = x.astype(BF16)
    r = x - hi.astype(F32)
    mid = r.astype(BF16)
    lo = (r - mid.astype(F32)).astype(BF16)
    return hi, mid, lo


def _dot_exact(x, m01):
    return sum(jnp.dot(p, m01, preferred_element_type=F32) for p in _split3(x))


def _transpose_exact(eye, x):
    return sum(lax.dot_general(eye, p, NT, preferred_element_type=F32) for p in _split3(x))


def _log_sigmoid(x):
    return jnp.minimum(x, 0.0) - jnp.log1p(jnp.exp(-jnp.abs(x)))


def _softcap(z):
    return GATE_CAP * jnp.tanh(z / GATE_CAP)


def _mlstm_prompt_kernel(q_ref, k_ref, v_ref, o_ref, gt_ref, bias_ref, gh_ref,
                         hc_ref, c_out, n_out, m_out, c_sc, n_sc, m_sc, *, heads, dk, dv):
    c = pl.program_id(1)
    L = q_ref.shape[0]
    scale = dk ** -0.5

    @pl.when(c == 0)
    def _():
        c_sc[...] = jnp.zeros_like(c_sc)
        n_sc[...] = jnp.zeros_like(n_sc)
        m_sc[...] = jnp.zeros_like(m_sc)

    z = _softcap(gt_ref[...] + bias_ref[...])
    li_r = z[:heads]
    lf_r = _log_sigmoid(z[heads:])
    row = lax.broadcasted_iota(jnp.int32, (L, L), 0)
    col = lax.broadcasted_iota(jnp.int32, (L, L), 1)
    causal = col <= row
    upper = (row <= col).astype(BF16)
    eye = (row == col).astype(BF16)
    b_r = _dot_exact(lf_r, upper)
    cols = _transpose_exact(eye, jnp.concatenate([li_r, b_r], axis=0))

    for h in range(heads):
        q = q_ref[:, h * dk:(h + 1) * dk]
        k = k_ref[:, h * dk:(h + 1) * dk]
        v = v_ref[:, h * dv:(h + 1) * dv]
        li_c = cols[:, h:h + 1]
        b_c = cols[:, heads + h:heads + h + 1]
        m_prev = m_sc[h:h + 1, 0:1]
        n_prev = n_sc[h:h + 1, :]
        c_prev = c_sc[h]

        logw = jnp.where(causal, b_c - b_r[h:h + 1, :] + li_r[h:h + 1, :], -jnp.inf)
        log_inter = b_c + m_prev
        m_t = jnp.maximum(log_inter, jnp.max(logw, axis=1, keepdims=True))
        s = lax.dot_general(q, k, NT, preferred_element_type=F32) * scale * jnp.exp(logw - m_t)
        w_inter = jnp.exp(log_inter - m_t)
        num = (jnp.dot(s.astype(BF16), v, preferred_element_type=F32)
               + w_inter * jnp.dot(q, c_prev.astype(BF16), preferred_element_type=F32))
        qn = jnp.sum(q.astype(F32) * n_prev, axis=1, keepdims=True)
        den = jnp.sum(s, axis=1, keepdims=True) + w_inter * qn
        hh = num / jnp.maximum(jnp.abs(den), jnp.exp(-m_t))
        hh = hh * lax.rsqrt(jnp.mean(hh * hh, axis=1, keepdims=True) + EPS)
        og = o_ref[:, h * dv:(h + 1) * dv].astype(F32)
        hc_ref[:, h * dv:(h + 1) * dv] = (
            hh * gh_ref[:, h * dv:(h + 1) * dv] * jax.nn.sigmoid(og)).astype(hc_ref.dtype)

        m_end = m_t[L - 1:L, :]
        b_last = b_c[L - 1:L, :]
        w_end = jnp.exp(b_last - b_c + li_c - m_end)
        decay = jnp.exp(b_last + m_prev - m_end)
        kw = k.astype(F32) * (scale * w_end)
        c_sc[h] = decay * c_prev + lax.dot_general(kw.astype(BF16), v, TN,
                                                   preferred_element_type=F32)
        n_sc[h:h + 1, :] = decay * n_prev + jnp.sum(kw, axis=0, keepdims=True)
        m_sc[h:h + 1, :] = jnp.broadcast_to(m_end, (1, m_sc.shape[1]))

    @pl.when(c == pl.num_programs(1) - 1)
    def _():
        c_out[0] = c_sc[...]
        n_out[0] = n_sc[...]
        m_out[0] = m_sc[...]


def mlstm_prompt(proj, gates, bias, g_head, batch, seq, heads, dk, dv, chunk):
    hk, hv = heads * dk, heads * dv
    nc = seq // chunk
    row = lambda b, c: b * nc + c
    return pl.pallas_call(
        functools.partial(_mlstm_prompt_kernel, heads=heads, dk=dk, dv=dv),
        grid=(batch, nc),
        in_specs=[pl.BlockSpec((chunk, hk), lambda b, c: (row(b, c), 0)),
                  pl.BlockSpec((chunk, hk), lambda b, c: (row(b, c), 1)),
                  pl.BlockSpec((chunk, hv), lambda b, c: (row(b, c), 2 * hk // hv)),
                  pl.BlockSpec((chunk, hv), lambda b, c: (row(b, c), 2 * hk // hv + 1)),
                  pl.BlockSpec((2 * heads, chunk), lambda b, c: (0, row(b, c))),
                  pl.BlockSpec((2 * heads, 1), lambda b, c: (0, 0)),
                  pl.BlockSpec((1, hv), lambda b, c: (0, 0))],
        out_specs=[pl.BlockSpec((chunk, hv), lambda b, c: (row(b, c), 0)),
                   pl.BlockSpec((1, heads, dk, dv), lambda b, c: (b, 0, 0, 0)),
                   pl.BlockSpec((1, heads, dk), lambda b, c: (b, 0, 0)),
                   pl.BlockSpec((1, heads, LANES_V7X), lambda b, c: (b, 0, 0))],
        out_shape=[jax.ShapeDtypeStruct((proj.shape[0], hv), BF16),
                   jax.ShapeDtypeStruct((batch, heads, dk, dv), F32),
                   jax.ShapeDtypeStruct((batch, heads, dk), F32),
                   jax.ShapeDtypeStruct((batch, heads, LANES_V7X), F32)],
        scratch_shapes=[pltpu.VMEM((heads, dk, dv), F32),
                        pltpu.VMEM((heads, dk), F32),
                        pltpu.VMEM((heads, LANES_V7X), F32)],
        compiler_params=_params("arbitrary", "arbitrary"),
        name="mlstm_prompt",
    )(proj, proj, proj, proj, gates, bias, g_head)


def _mlstm_sample_kernel(q_ref, k_ref, v_ref, o_ref, gt_ref, bias_ref, gh_ref, m0_ref,
                         n0_ref, c0_ref, hc_prev_ref, hc_ref, c_out, n_out, m_out,
                         *, heads, dk, dv, ts):
    del hc_prev_ref
    R = q_ref.shape[0]
    G = R // ts
    scale = dk ** -0.5

    z = _softcap(gt_ref[0] + bias_ref[...])
    li_r = z[:heads]
    lf_r = _log_sigmoid(z[heads:])
    row = lax.broadcasted_iota(jnp.int32, (R, R), 0)
    col = lax.broadcasted_iota(jnp.int32, (R, R), 1)
    seq_id = lambda i: sum((i >= g * ts).astype(jnp.int32) for g in range(1, G))
    same = seq_id(row) == seq_id(col)
    causal = same & (col <= row)
    upper = (same & (row <= col)).astype(BF16)
    eye = (row == col).astype(BF16)
    b_r = _dot_exact(lf_r, upper)
    cols = _transpose_exact(eye, jnp.concatenate([li_r, b_r], axis=0))
    seq_of_row = seq_id(lax.broadcasted_iota(jnp.int32, (R, 1), 0))
    m0 = m0_ref[0]

    for h in range(heads):
        q = q_ref[:, h * dk:(h + 1) * dk]
        k = k_ref[:, h * dk:(h + 1) * dk]
        v = v_ref[:, h * dv:(h + 1) * dv]
        li_c = cols[:, h:h + 1]
        b_c = cols[:, heads + h:heads + h + 1]
        m_prev = m0[:, h:h + 1]

        logw = jnp.where(causal, b_c - b_r[h:h + 1, :] + li_r[h:h + 1, :], -jnp.inf)
        log_inter = b_c + m_prev
        m_t = jnp.maximum(log_inter, jnp.max(logw, axis=1, keepdims=True))
        s = lax.dot_general(q, k, NT, preferred_element_type=F32) * scale * jnp.exp(logw - m_t)
        w_inter = jnp.exp(log_inter - m_t)

        qc = jnp.zeros((R, dv), F32)
        qn = jnp.zeros((R, 1), F32)
        qf = q.astype(F32)
        for g in range(G):
            mine = seq_of_row == g
            qc = jnp.where(mine, jnp.dot(q, c0_ref[g, h].astype(BF16),
                                         preferred_element_type=F32), qc)
            qn = jnp.where(mine, jnp.sum(qf * n0_ref[g, h:h + 1, :], axis=1, keepdims=True), qn)
        num = jnp.dot(s.astype(BF16), v, preferred_element_type=F32) + w_inter * qc
        den = jnp.sum(s, axis=1, keepdims=True) + w_inter * qn
        hh = num / jnp.maximum(jnp.abs(den), jnp.exp(-m_t))
        hh = hh * lax.rsqrt(jnp.mean(hh * hh, axis=1, keepdims=True) + EPS)
        og = o_ref[:, h * dv:(h + 1) * dv].astype(F32)
        hc_ref[:, h * dv:(h + 1) * dv] = (
            hh * gh_ref[:, h * dv:(h + 1) * dv] * jax.nn.sigmoid(og)).astype(hc_ref.dtype)

        kf = k.astype(F32)
        for g in range(G):
            last = (g + 1) * ts - 1
            m_end = m_t[last:last + 1, :]
            b_last = b_c[last:last + 1, :]
            mine = seq_of_row == g
            w_end = jnp.where(mine, jnp.exp(b_last - b_c + li_c - m_end), 0.0)
            decay = jnp.exp(b_last + m_prev[last:last + 1, :] - m_end)
            kw = kf * (scale * w_end)
            c_out[g, h] = decay * c0_ref[g, h] + lax.dot_general(
                kw.astype(BF16), v, TN, preferred_element_type=F32)
            n_out[g, h:h + 1, :] = decay * n0_ref[g, h:h + 1, :] + jnp.sum(kw, axis=0, keepdims=True)
            m_out[g, h:h + 1, :] = jnp.broadcast_to(m_end, (1, m_out.shape[2]))


def mlstm_sample(proj, gates_blk, bias, g_head, m0_blk, n0, c0, hc_buf, row_start,
                 heads, dk, dv, ts, group):
    hk, hv = heads * dk, heads * dv
    rows = group * ts
    nb = c0.shape[0]
    steps = nb // group
    assert row_start % rows == 0
    first = row_start // rows
    return pl.pallas_call(
        functools.partial(_mlstm_sample_kernel, heads=heads, dk=dk, dv=dv, ts=ts),
        grid=(steps,),
        in_specs=[pl.BlockSpec((rows, hk), lambda i: (i + first, 0)),
                  pl.BlockSpec((rows, hk), lambda i: (i + first, 1)),
                  pl.BlockSpec((rows, hv), lambda i: (i + first, 2 * hk // hv)),
                  pl.BlockSpec((rows, hv), lambda i: (i + first, 2 * hk // hv + 1)),
                  pl.BlockSpec((1, 2 * heads, rows), lambda i: (i, 0, 0)),
                  pl.BlockSpec((2 * heads, 1), lambda i: (0, 0)),
                  pl.BlockSpec((1, hv), lambda i: (0, 0)),
                  pl.BlockSpec((1, rows, heads), lambda i: (i, 0, 0)),
                  pl.BlockSpec((group, heads, dk), lambda i: (i, 0, 0)),
                  pl.BlockSpec((group, heads, dk, dv), lambda i: (i, 0, 0, 0)),
                  pl.BlockSpec(memory_space=pl.ANY)],
        out_specs=[pl.BlockSpec((rows, hv), lambda i: (i + first, 0)),
                   pl.BlockSpec((group, heads, dk, dv), lambda i: (i, 0, 0, 0)),
                   pl.BlockSpec((group, heads, dk), lambda i: (i, 0, 0)),
                   pl.BlockSpec((group, heads, LANES_V7X), lambda i: (i, 0, 0))],
        out_shape=[jax.ShapeDtypeStruct(hc_buf.shape, hc_buf.dtype),
                   jax.ShapeDtypeStruct((nb, heads, dk, dv), F32),
                   jax.ShapeDtypeStruct((nb, heads, dk), F32),
                   jax.ShapeDtypeStruct((nb, heads, LANES_V7X), F32)],
        input_output_aliases={10: 0},
        compiler_params=_params("arbitrary"),
        name="mlstm_sample",
    )(proj, proj, proj, proj, gates_blk, bias, g_head, m0_blk, n0, c0, hc_buf)


def _online_softmax(s, m_prev, c_exp):
    m_new = jnp.maximum(m_prev, jnp.max(s, axis=1, keepdims=True))
    alpha = jnp.exp2((m_prev - m_new) * c_exp)
    p = jnp.exp2((s - m_new) * c_exp)
    return m_new, alpha, p


def _attn_prompt_kernel(q_ref, kn_ref, kr_ref, v_ref, o_ref, kf_sc, vx_sc, m_sc, acc_sc,
                        *, c_exp, nope, vdim, parts):
    qi = pl.program_id(2)
    tq = q_ref.shape[0]
    tp = tq // parts

    @pl.when(qi == 0)
    def _():
        kf_sc[:, :nope] = kn_ref[...]
        kf_sc[:, nope:] = kr_ref[...]
        vx_sc[:, :vdim] = v_ref[...]
        lane = lax.broadcasted_iota(jnp.int32, (vx_sc.shape[0], vx_sc.shape[1] - vdim), 1)
        vx_sc[:, vdim:] = (lane == 0).astype(BF16)

    m_sc[...] = jnp.full_like(m_sc, -jnp.inf)
    acc_sc[...] = jnp.zeros_like(acc_sc)

    def update(part, k, vx, row_offset):
        s = lax.dot_general(q_ref[part * tp:(part + 1) * tp, :], k, NT, preferred_element_type=F32)
        if row_offset is not None:
            row = lax.broadcasted_iota(jnp.int32, s.shape, 0) + row_offset
            col = lax.broadcasted_iota(jnp.int32, s.shape, 1)
            s = jnp.where(col <= row, s, -jnp.inf)
        m_new, alpha, p = _online_softmax(s, m_sc[part], c_exp)
        acc_sc[part] = alpha * acc_sc[part] + jnp.dot(p.astype(BF16), vx,
                                                      preferred_element_type=F32)
        m_sc[part] = m_new

    def body(ki, carry):
        start = pl.multiple_of(ki * tq, tq)
        k = kf_sc[pl.ds(start, tq), :]
        vx = vx_sc[pl.ds(start, tq), :]
        for part in range(parts):
            update(part, k, vx, None)
        return carry

    lax.fori_loop(0, qi, body, 0)
    start = pl.multiple_of(qi * tq, tq)
    for part in range(parts):
        n_keys = (part + 1) * tp
        update(part, kf_sc[pl.ds(start, n_keys), :], vx_sc[pl.ds(start, n_keys), :], part * tp)
    for part in range(parts):
        acc = acc_sc[part]
        o_ref[part * tp:(part + 1) * tp, :] = (acc[:, :vdim] / acc[:, vdim:vdim + 1]).astype(o_ref.dtype)


def attn_prompt(q, kn, kr, v, batch, seq, heads, nope, vdim, scale, tq, parts):
    hw = q.shape[1] // heads
    nq = seq // tq
    return pl.pallas_call(
        functools.partial(_attn_prompt_kernel, c_exp=scale * LOG2E, nope=nope, vdim=vdim,
                          parts=parts),
        grid=(batch, heads, nq),
        in_specs=[pl.BlockSpec((tq, hw), lambda b, h, i: (b * nq + i, h)),
                  pl.BlockSpec((seq, nope), lambda b, h, i: (b, h)),
                  pl.BlockSpec((seq, hw - nope), lambda b, h, i: (b, 0)),
                  pl.BlockSpec((seq, vdim), lambda b, h, i: (b, h))],
        out_specs=pl.BlockSpec((tq, vdim), lambda b, h, i: (b * nq + i, h)),
        out_shape=jax.ShapeDtypeStruct((q.shape[0], heads * vdim), BF16),
        scratch_shapes=[pltpu.VMEM((seq, hw), BF16),
                        pltpu.VMEM((seq, 2 * vdim), BF16),
                        pltpu.VMEM((parts, tq // parts, 1), F32),
                        pltpu.VMEM((parts, tq // parts, 2 * vdim), F32)],
        compiler_params=_params("arbitrary", "arbitrary", "arbitrary"),
        name="attn_prompt",
    )(q, kn, kr, v)


def _qlat_kernel(q_ref, w_ref, o_ref, *, nope):
    q = q_ref[...]
    lat = lax.dot_general(q[:, :nope], w_ref[...].astype(BF16), NT, preferred_element_type=F32)
    o_ref[0] = jnp.concatenate([lat.astype(BF16), q[:, nope:]], axis=1)


def q_latent(q, w_uk2, heads, nope, row_start, n_rows):
    m = n_rows
    assert row_start % m == 0
    first = row_start // m
    hw = q.shape[1] // heads
    rank = w_uk2.shape[0]
    return pl.pallas_call(
        functools.partial(_qlat_kernel, nope=nope),
        grid=(heads,),
        in_specs=[pl.BlockSpec((m, hw), lambda h: (first, h)),
                  pl.BlockSpec((rank, nope), lambda h: (0, h))],
        out_specs=pl.BlockSpec((1, m, rank + hw - nope), lambda h: (h, 0, 0)),
        out_shape=jax.ShapeDtypeStruct((heads, m, rank + hw - nope), BF16),
        compiler_params=_params("arbitrary"),
        name="q_latent",
    )(q, w_uk2)


def _decode_kernel(pt_ref, q_ref, new_ref, *rest, pages, c_exp, rank, rope, ts):
    ck_refs = rest[:pages]
    kr_refs = rest[pages:2 * pages]
    o_ref, kc_sc, krt_sc, m_sc, l_sc, acc_sc = rest[2 * pages:]
    j = pl.program_id(1)
    psz = ck_refs[0].shape[0]

    @pl.when(j == 0)
    def _():
        m_sc[...] = jnp.full_like(m_sc, -jnp.inf)
        l_sc[...] = jnp.zeros_like(l_sc)
        acc_sc[...] = jnp.zeros_like(acc_sc)

    for p in range(pages):
        kc_sc[p * psz:(p + 1) * psz, :] = ck_refs[p][...].astype(BF16)
        krt_sc[:, p * psz:(p + 1) * psz] = kr_refs[p][...].astype(BF16)

    q = q_ref[0]
    s = (lax.dot_general(q[:, :rank], kc_sc[...], NT, preferred_element_type=F32)
         + jnp.dot(q[:, rank:rank + rope], krt_sc[...], preferred_element_type=F32))
    m, alpha, p = _online_softmax(s, m_sc[...], c_exp)
    l = alpha * l_sc[...] + jnp.sum(p, axis=1, keepdims=True)
    acc = alpha * acc_sc[...] + jnp.dot(p.astype(BF16), kc_sc[...], preferred_element_type=F32)
    m_sc[...] = m
    l_sc[...] = l
    acc_sc[...] = acc

    @pl.when(j == pl.num_programs(1) - 1)
    def _():
        new = new_ref[0]
        s2 = lax.dot_general(q, new, NT, preferred_element_type=F32)
        qrow = lax.broadcasted_iota(jnp.int32, s2.shape, 0)
        heads = q.shape[0] // ts
        tok = sum((qrow >= t * heads).astype(jnp.int32) for t in range(1, ts))
        key = lax.broadcasted_iota(jnp.int32, s2.shape, 1)
        s2 = jnp.where(key <= tok, s2, -jnp.inf)
        _, alpha, p = _online_softmax(s2, m, c_exp)
        l2 = alpha * l + jnp.sum(p, axis=1, keepdims=True)
        acc2 = alpha * acc + jnp.dot(p.astype(BF16), new[:, :rank], preferred_element_type=F32)
        o_ref[0] = acc2 / l2


def attn_decode(q_full, new_keys, cache_ckv, cache_krope_t, page_table, scale, ts, pages):
    nb, rows, width = q_full.shape
    _, psz, rank = cache_ckv.shape
    rope = cache_krope_t.shape[1]
    n_pages = page_table.shape[1]
    nj = n_pages // pages

    def page_spec(p, shape):
        return pl.BlockSpec((None,) + shape, lambda b, j, pt: (pt[b, j * pages + p], 0, 0))

    grid_spec = pltpu.PrefetchScalarGridSpec(
        num_scalar_prefetch=1,
        grid=(nb, nj),
        in_specs=([pl.BlockSpec((1, rows, width), lambda b, j, pt: (b, 0, 0)),
                   pl.BlockSpec((1, new_keys.shape[1], width), lambda b, j, pt: (b, 0, 0))]
                  + [page_spec(p, (psz, rank)) for p in range(pages)]
                  + [page_spec(p, (rope, psz)) for p in range(pages)]),
        out_specs=pl.BlockSpec((1, rows, rank), lambda b, j, pt: (b, 0, 0)),
        scratch_shapes=[pltpu.VMEM((pages * psz, rank), BF16),
                        pltpu.VMEM((rope, pages * psz), BF16),
                        pltpu.VMEM((rows, 1), F32),
                        pltpu.VMEM((rows, 1), F32),
                        pltpu.VMEM((rows, rank), F32)],
    )
    return pl.pallas_call(
        functools.partial(_decode_kernel, pages=pages, c_exp=scale * LOG2E, rank=rank,
                          rope=rope, ts=ts),
        grid_spec=grid_spec,
        out_shape=jax.ShapeDtypeStruct((nb, rows, rank), F32),
        compiler_params=_params("arbitrary", "arbitrary"),
        name="attn_decode",
    )(page_table, q_full, new_keys, *([cache_ckv] * pages), *([cache_krope_t] * pages))


def _oup_kernel(o_ref, w_ref, prev_ref, out_ref):
    del prev_ref
    out_ref[...] = jnp.dot(o_ref[0], w_ref[...].astype(BF16),
                           preferred_element_type=F32).astype(out_ref.dtype)


def o_up(o_lat, w_uv2, vdim, o_buf, row_start):
    heads, m, rank = o_lat.shape
    assert row_start % m == 0
    first = row_start // m
    return pl.pallas_call(
        _oup_kernel,
        grid=(heads,),
        in_specs=[pl.BlockSpec((1, m, rank), lambda h: (h, 0, 0)),
                  pl.BlockSpec((rank, vdim), lambda h: (0, h)),
                  pl.BlockSpec(memory_space=pl.ANY)],
        out_specs=pl.BlockSpec((m, vdim), lambda h: (first, h)),
        out_shape=jax.ShapeDtypeStruct(o_buf.shape, o_buf.dtype),
        input_output_aliases={2: 0},
        compiler_params=_params("arbitrary"),
        name="o_up",
    )(o_lat, w_uv2, o_buf)


def _rope_tables(pos, rope_dim, nope_pad, width):
    half = rope_dim // 2
    freqs = ROPE_THETA ** (-jnp.arange(half, dtype=F32) / half)
    ang = pos.astype(F32)[:, None] * freqs[None, :]
    cos, sin = jnp.cos(ang), jnp.sin(ang)
    n = pos.shape[0]
    zeros = lambda w: jnp.zeros((n, w), F32)
    tail = width - nope_pad - rope_dim
    ta = jnp.concatenate([jnp.ones((n, nope_pad), F32), cos, cos, zeros(tail)], axis=1)
    tb = jnp.concatenate([zeros(nope_pad + half), sin, zeros(tail)], axis=1)
    tc = jnp.concatenate([zeros(nope_pad), -sin, zeros(half + tail)], axis=1)
    return ta, tb, tc


def kernel(x_prompt, x_sample, cache_ckv, cache_krope, page_table, state_C, state_n, state_m,
           g_mix, g_ffn, w_ffn_in, w_ffn_out, w_a_in, b_a_gates, g_a_head, w_a_out,
           g_kv_in, w_kv_down, g_kv_latent, w_uk, w_uv, w_q_down, g_q_latent, w_q_up,
           w_b_out, g_final):
    bp, tp, d = x_prompt.shape
    bs, ts, _ = x_sample.shape
    mp, ms = bp * tp, bs * ts
    n_a = state_C.shape[0]
    n_layers = g_mix.shape[0]
    assert n_a == 1 and n_layers == 2, "one mLSTM layer followed by one MLA layer"
    _, _, heads_a, dk, dv = state_C.shape
    hk, hv = heads_a * dk, heads_a * dv
    rank, heads_b, nope = w_uk.shape
    vdim = w_uv.shape[2]
    rope = cache_krope.shape[2]
    psz = cache_ckv.shape[1]
    past = page_table.shape[1] * psz
    hw = nope + 2 * rope
    assert nope % LANES_V7X == 0 and hw % LANES_V7X == 0 and (2 * rope) % LANES_V7X == 0
    attn_scale = (nope + rope) ** -0.5

    x = jnp.concatenate([x_prompt.reshape(mp, d), x_sample.reshape(ms, d)], axis=0)
    pos = jnp.concatenate([jnp.tile(jnp.arange(tp), bp), jnp.tile(past + jnp.arange(ts), bs)])
    tabs = _rope_tables(pos, rope, nope, hw)

    (xn,) = rmsnorm_rows(x, g_mix[0:1])
    w_in_t = jnp.swapaxes(w_a_in[0], 0, 1)
    proj = matmul(xn, w_in_t, n_out=2 * hk + 2 * hv, out_dtype=BF16, w_t=True,
                  name="mlstm_in_proj")
    gates = gates_t(xn, w_in_t[2 * hk + 2 * hv:])
    bias = b_a_gates[0].reshape(2 * heads_a, 1)
    g_head = g_a_head[0].reshape(1, hv)

    hc, c_p, n_p, m_p = mlstm_prompt(proj, gates, bias, g_head, bp, tp, heads_a, dk, dv,
                                     chunk=min(256, tp))
    group = 4
    rows = group * ts
    gates_blk = gates[:, mp:].reshape(2 * heads_a, bs // group, rows).transpose(1, 0, 2)
    m0_blk = jnp.repeat(state_m[0], ts, axis=0).reshape(bs // group, rows, heads_a)
    hc, c_s, n_s, m_s = mlstm_sample(proj, gates_blk, bias, g_head, m0_blk, state_n[0],
                                     state_C[0], hc, mp, heads_a, dk, dv, ts, group)
    x = matmul(hc, w_a_out[0], res=x, name="mlstm_out_proj")

    (xn,) = rmsnorm_rows(x, g_ffn[0:1])
    x = matmul(swiglu_in(xn, w_ffn_in[0]), w_ffn_out[0], res=x, name="ffn_out")

    xn_kv, xn_q = rmsnorm_rows(x, jnp.stack([g_kv_in, g_mix[1]]))
    w_kv_pad = jnp.pad(w_kv_down, ((0, 0), (0, rope)))
    g_lat = g_kv_latent.reshape(1, rank)
    ckv_p, ckvb_p, kr_p, krb_p = kv_down(xn_kv, w_kv_pad, g_lat, tabs, rank, rope, 0, mp)
    ckv_s, ckvb_s, kr_s, krb_s = kv_down(xn_kv, w_kv_pad, g_lat, tabs, rank, rope, mp, ms)

    xq = q_down(xn_q, w_q_down[0], g_q_latent[0].reshape(1, rank))
    w_qup_pad = jnp.pad(w_q_up[0].reshape(rank, heads_b, nope + rope),
                        ((0, 0), (0, 0), (0, rope))).reshape(rank, heads_b * hw)
    q = q_up(xq, w_qup_pad, tabs, hw)

    w_uk2 = w_uk.reshape(rank, heads_b * nope)
    w_uv2 = w_uv.reshape(rank, heads_b * vdim)
    kn = matmul(ckvb_p, w_uk2, out_dtype=BF16, name="k_up")
    v = matmul(ckvb_p, w_uv2, out_dtype=BF16, name="v_up")
    tq = min(2048, tp)
    o = attn_prompt(q, kn, krb_p, v, bp, tp, heads_b, nope, vdim, attn_scale,
                    tq=tq, parts=max(1, tq // 256))

    q_lat = q_latent(q, w_uk2, heads_b, nope, mp, ms)
    width = rank + 2 * rope
    q_full = q_lat.reshape(heads_b, bs, ts, width).transpose(1, 2, 0, 3).reshape(bs, ts * heads_b, width)
    new_keys = jnp.concatenate([ckvb_s, krb_s], axis=1).reshape(bs, ts, width)
    new_keys = jnp.pad(new_keys, ((0, 0), (0, BF16_SUBLANES_V7X - ts), (0, 0)))
    cache_krope_t = jnp.swapaxes(cache_krope, 1, 2)
    o_lat = attn_decode(q_full, new_keys, cache_ckv, cache_krope_t, page_table, attn_scale, ts,
                        pages=min(16, page_table.shape[1]))
    o_lat = o_lat.reshape(bs, ts, heads_b, rank).transpose(2, 0, 1, 3).reshape(heads_b, ms, rank)
    o = o_up(o_lat.astype(BF16), w_uv2, vdim, o, mp)

    x = matmul(o, w_b_out[0], res=x, name="mla_out_proj")

    (xn,) = rmsnorm_rows(x, g_ffn[1:2])
    x = matmul(swiglu_in(xn, w_ffn_in[1]), w_ffn_out[1], res=x, name="ffn_out")
    g_fin = g_final.reshape(1, d)
    (y_p,) = rmsnorm_rows(x, g_fin, out_dtype=F32, row_start=0, n_rows=mp)
    (y_s,) = rmsnorm_rows(x, g_fin, out_dtype=F32, row_start=mp, n_rows=ms)

    return (y_p.reshape(bp, tp, d), y_s.reshape(bs, ts, d),
            c_p[None], n_p[None], m_p[None, :, :, 0],
            ckv_p.reshape(bp, tp, rank), kr_p.reshape(bp, tp, rope),
            c_s[None], n_s[None], m_s[None, :, :, 0],
            ckv_s.reshape(bs, ts, rank), kr_s.reshape(bs, ts, rope))
```

```python
import functools

import jax
import jax.numpy as jnp
from jax import lax
from jax.experimental import pallas as pl
from jax.experimental.pallas import tpu as pltpu

F32 = jnp.float32
BF16 = jnp.bfloat16

EPS = 1e-6
GATE_CAP = 15.0
ROPE_THETA = 10000.0
LOG2E = 1.4426950408889634

LANES_V7X = 128
BF16_SUBLANES_V7X = 16
VMEM_BYTES_V7X = 64 * 1024 * 1024
VMEM_BUDGET = VMEM_BYTES_V7X * 5 // 8
VMEM_LIMIT = VMEM_BYTES_V7X * 7 // 8

NT = (((1,), (1,)), ((), ()))
TN = (((0,), (0,)), ((), ()))


def _params(*sem):
    return pltpu.CompilerParams(dimension_semantics=sem, vmem_limit_bytes=VMEM_LIMIT)


def _largest_divisor(n, cap, mult):
    best = None
    for d in range(mult, min(n, cap) + 1, mult):
        if n % d == 0:
            best = d
    return n if best is None else best


def _row_tile(m, cap=1088):
    return _largest_divisor(m, cap, BF16_SUBLANES_V7X)


def _rms_kernel(x_ref, g_ref, *o_refs):
    x = x_ref[...]
    y = x * lax.rsqrt(jnp.mean(x * x, axis=-1, keepdims=True) + EPS)
    for i, o_ref in enumerate(o_refs):
        o_ref[...] = (y * g_ref[i:i + 1, :]).astype(o_ref.dtype)


def rmsnorm_rows(x, gains, out_dtype=BF16, row_start=0, n_rows=None):
    d = x.shape[1]
    m = x.shape[0] if n_rows is None else n_rows
    n = gains.shape[0]
    tm = _largest_divisor(m, 512, BF16_SUBLANES_V7X)
    assert row_start % tm == 0
    first = row_start // tm
    outs = pl.pallas_call(
        _rms_kernel,
        grid=(m // tm,),
        in_specs=[pl.BlockSpec((tm, d), lambda i: (i + first, 0)),
                  pl.BlockSpec((n, d), lambda i: (0, 0))],
        out_specs=[pl.BlockSpec((tm, d), lambda i: (i, 0))] * n,
        out_shape=[jax.ShapeDtypeStruct((m, d), out_dtype)] * n,
        compiler_params=_params("arbitrary"),
        name="rmsnorm_rows",
    )(x, gains)
    return outs


def _mm_tiles(m, k, n, out_bytes, n_w=1, has_res=False):
    best = None
    for tn in (512, 256, 128):
        if n % tn:
            continue
        for tm in range(BF16_SUBLANES_V7X, min(m, 1088) + 1, BF16_SUBLANES_V7X):
            if m % tm:
                continue
            need = (2 * tm * k * 2 + n_w * (2 * k * tn * 4 + k * tn * 2)
                    + 2 * tm * tn * out_bytes + (2 * tm * tn * 4 if has_res else 0))
            if need <= VMEM_BUDGET and (best is None or tm * tn > best[0] * best[1]):
                best = (tm, tn)
    assert best is not None, (m, k, n)
    return best


def _mm_kernel(a_ref, w_ref, *rest, has_res, w_t):
    if has_res:
        r_ref, o_ref, w_sc = rest
    else:
        o_ref, w_sc = rest

    @pl.when(pl.program_id(1) == 0)
    def _():
        w_sc[...] = w_ref[...].astype(BF16)

    if w_t:
        acc = lax.dot_general(a_ref[...], w_sc[...], NT, preferred_element_type=F32)
    else:
        acc = jnp.dot(a_ref[...], w_sc[...], preferred_element_type=F32)
    if has_res:
        acc = acc + r_ref[...]
    o_ref[...] = acc.astype(o_ref.dtype)


def _layer_spec(block, index_map, layer):
    if layer is None:
        return pl.BlockSpec(block, index_map)
    return pl.BlockSpec((None,) + block, lambda *g: (layer,) + index_map(*g))


def matmul(a, w, n_out=None, res=None, out_dtype=F32, w_t=False, layer=None, name="matmul"):
    m, k = a.shape
    w_shape = w.shape if layer is None else w.shape[1:]
    n = (w_shape[0] if w_t else w_shape[1]) if n_out is None else n_out
    tm, tn = _mm_tiles(m, k, n, jnp.dtype(out_dtype).itemsize, has_res=res is not None)
    w_block = (tn, k) if w_t else (k, tn)
    w_map = (lambda j, i: (j, 0)) if w_t else (lambda j, i: (0, j))
    in_specs = [pl.BlockSpec((tm, k), lambda j, i: (i, 0)),
                _layer_spec(w_block, w_map, layer)]
    args = [a, w]
    if res is not None:
        in_specs.append(pl.BlockSpec((tm, tn), lambda j, i: (i, j)))
        args.append(res)
    return pl.pallas_call(
        functools.partial(_mm_kernel, has_res=res is not None, w_t=w_t),
        grid=(n // tn, m // tm),
        in_specs=in_specs,
        out_specs=pl.BlockSpec((tm, tn), lambda j, i: (i, j)),
        out_shape=jax.ShapeDtypeStruct((m, n), out_dtype),
        scratch_shapes=[pltpu.VMEM(w_block, BF16)],
        compiler_params=_params("arbitrary", "arbitrary"),
        name=name,
    )(*args)


def _swiglu_kernel(a_ref, wg_ref, wu_ref, o_ref, wg_sc, wu_sc):
    @pl.when(pl.program_id(1) == 0)
    def _():
        wg_sc[...] = wg_ref[...].astype(BF16)
        wu_sc[...] = wu_ref[...].astype(BF16)

    a = a_ref[...]
    g = jnp.dot(a, wg_sc[...], preferred_element_type=F32)
    u = jnp.dot(a, wu_sc[...], preferred_element_type=F32)
    o_ref[...] = (g * jax.nn.sigmoid(g) * u).astype(o_ref.dtype)


def swiglu_in(a, w_in, layer):
    m, k = a.shape
    f = w_in.shape[2] // 2
    tm, tf = _mm_tiles(m, k, f, 2, n_w=2)
    nf = f // tf
    return pl.pallas_call(
        _swiglu_kernel,
        grid=(nf, m // tm),
        in_specs=[pl.BlockSpec((tm, k), lambda j, i: (i, 0)),
                  _layer_spec((k, tf), lambda j, i: (0, j), layer),
                  _layer_spec((k, tf), lambda j, i: (0, j + nf), layer)],
        out_specs=pl.BlockSpec((tm, tf), lambda j, i: (i, j)),
        out_shape=jax.ShapeDtypeStruct((m, f), BF16),
        scratch_shapes=[pltpu.VMEM((k, tf), BF16), pltpu.VMEM((k, tf), BF16)],
        compiler_params=_params("arbitrary", "arbitrary"),
        name="swiglu_in",
    )(a, w_in, w_in)


def _gates_kernel(wt_ref, a_ref, o_ref):
    o_ref[...] = lax.dot_general(wt_ref[...].astype(BF16), a_ref[...], NT,
                                 preferred_element_type=F32)


def gates_t(a, w_gate_t):
    m, k = a.shape
    g = w_gate_t.shape[0]
    tm = _largest_divisor(m, 512, LANES_V7X)
    return pl.pallas_call(
        _gates_kernel,
        grid=(m // tm,),
        in_specs=[pl.BlockSpec((g, k), lambda i: (0, 0)),
                  pl.BlockSpec((tm, k), lambda i: (i, 0))],
        out_specs=pl.BlockSpec((g, tm), lambda i: (0, i)),
        out_shape=jax.ShapeDtypeStruct((g, m), F32),
        compiler_params=_params("arbitrary"),
        name="gates_t",
    )(w_gate_t, a)


def _rope_lanes(x, a, b, c):
    w = x.shape[1]
    return x * a + pltpu.roll(x, 32, 1) * b + pltpu.roll(x, w - 32, 1) * c


def _qdown_kernel(a_ref, w_ref, g_ref, o_ref, w_sc):
    @pl.when(pl.program_id(0) == 0)
    def _():
        w_sc[...] = w_ref[...].astype(BF16)

    y = jnp.dot(a_ref[...], w_sc[...], preferred_element_type=F32)
    y = y * lax.rsqrt(jnp.mean(y * y, axis=-1, keepdims=True) + EPS)
    o_ref[...] = (y * g_ref[...]).astype(o_ref.dtype)


def q_down(a, w, g):
    m, k = a.shape
    r = w.shape[1]
    tm = _row_tile(m)
    return pl.pallas_call(
        _qdown_kernel,
        grid=(m // tm,),
        in_specs=[pl.BlockSpec((tm, k), lambda i: (i, 0)),
                  pl.BlockSpec((k, r), lambda i: (0, 0)),
                  pl.BlockSpec((1, r), lambda i: (0, 0))],
        out_specs=pl.BlockSpec((tm, r), lambda i: (i, 0)),
        out_shape=jax.ShapeDtypeStruct((m, r), BF16),
        scratch_shapes=[pltpu.VMEM((k, r), BF16)],
        compiler_params=_params("arbitrary"),
        name="q_down",
    )(a, w, g)


def _qup_kernel(a_ref, w_ref, ta_ref, tb_ref, tc_ref, o_ref, w_sc, *, hw):
    @pl.when(pl.program_id(0) == 0)
    def _():
        w_sc[...] = w_ref[...].astype(BF16)

    a = a_ref[...]
    ta, tb, tc = ta_ref[...], tb_ref[...], tc_ref[...]
    for h in range(w_sc.shape[1] // hw):
        sl = slice(h * hw, (h + 1) * hw)
        y = jnp.dot(a, w_sc[:, sl], preferred_element_type=F32)
        o_ref[:, sl] = _rope_lanes(y, ta, tb, tc).astype(o_ref.dtype)


def q_up(a, w_pad, tabs, hw):
    m, k = a.shape
    n = w_pad.shape[1]
    tm = _row_tile(m, 544)
    tspec = pl.BlockSpec((tm, hw), lambda i: (i, 0))
    return pl.pallas_call(
        functools.partial(_qup_kernel, hw=hw),
        grid=(m // tm,),
        in_specs=[pl.BlockSpec((tm, k), lambda i: (i, 0)),
                  pl.BlockSpec((k, n), lambda i: (0, 0)),
                  tspec, tspec, tspec],
        out_specs=pl.BlockSpec((tm, n), lambda i: (i, 0)),
        out_shape=jax.ShapeDtypeStruct((m, n), BF16),
        scratch_shapes=[pltpu.VMEM((k, n), BF16)],
        compiler_params=_params("arbitrary"),
        name="q_up",
    )(a, w_pad, *tabs)


def _kvdown_kernel(a_ref, w_ref, g_ref, ta_ref, tb_ref, tc_ref,
                   ckv_ref, ckvb_ref, kr_ref, krb_ref, w_sc, *, rank):
    @pl.when(pl.program_id(0) == 0)
    def _():
        w_sc[...] = w_ref[...].astype(BF16)

    y = jnp.dot(a_ref[...], w_sc[...], preferred_element_type=F32)
    lat = y[:, :rank]
    lat = lat * lax.rsqrt(jnp.mean(lat * lat, axis=-1, keepdims=True) + EPS) * g_ref[...]
    ckv_ref[...] = lat
    ckvb_ref[...] = lat.astype(BF16)
    kr = _rope_lanes(y[:, rank:], ta_ref[...], tb_ref[...], tc_ref[...])
    kr_ref[...] = kr[:, :kr_ref.shape[1]]
    krb_ref[...] = kr.astype(BF16)


def kv_down(a, w_pad, g, tabs, rank, rope, row_start, n_rows):
    k = a.shape[1]
    m = n_rows
    n = w_pad.shape[1]
    rw = n - rank
    tm = _row_tile(m)
    assert row_start % tm == 0
    first = row_start // tm
    tspec = pl.BlockSpec((tm, rw), lambda i: (i + first, 1))
    return pl.pallas_call(
        functools.partial(_kvdown_kernel, rank=rank),
        grid=(m // tm,),
        in_specs=[pl.BlockSpec((tm, k), lambda i: (i + first, 0)),
                  pl.BlockSpec((k, n), lambda i: (0, 0)),
                  pl.BlockSpec((1, rank), lambda i: (0, 0)),
                  tspec, tspec, tspec],
        out_specs=[pl.BlockSpec((tm, rank), lambda i: (i, 0)),
                   pl.BlockSpec((tm, rank), lambda i: (i, 0)),
                   pl.BlockSpec((tm, rope), lambda i: (i, 0)),
                   pl.BlockSpec((tm, rw), lambda i: (i, 0))],
        out_shape=[jax.ShapeDtypeStruct((m, rank), F32),
                   jax.ShapeDtypeStruct((m, rank), BF16),
                   jax.ShapeDtypeStruct((m, rope), F32),
                   jax.ShapeDtypeStruct((m, rw), BF16)],
        scratch_shapes=[pltpu.VMEM((k, n), BF16)],
        compiler_params=_params("arbitrary"),
        name="kv_down",
    )(a, w_pad, g, *tabs)


def _split3(x):
    hi = x.astype(BF16)
    r = x - hi.astype(F32)
    mid = r.astype(BF16)
    lo = (r - mid.astype(F32)).astype(BF16)
    return hi, mid, lo


def _dot_exact(x, m01):
    return sum(jnp.dot(p, m01, preferred_element_type=F32) for p in _split3(x))


def _transpose_exact(eye, x):
    return sum(lax.dot_general(eye, p, NT, preferred_element_type=F32) for p in _split3(x))


def _log_sigmoid(x):
    return jnp.minimum(x, 0.0) - jnp.log1p(jnp.exp(-jnp.abs(x)))


def _softcap(z):
    return GATE_CAP * jnp.tanh(z / GATE_CAP)


def _mlstm_prompt_kernel(q_ref, k_ref, v_ref, o_ref, gt_ref, bias_ref, gh_ref, hc_prev_ref,
                         hc_ref, c_out, n_out, m_out, c_sc, n_sc, m_sc, *, heads, dk, dv):
    del hc_prev_ref
    c = pl.program_id(1)
    L = q_ref.shape[0]
    scale = dk ** -0.5

    @pl.when(c == 0)
    def _():
        c_sc[...] = jnp.zeros_like(c_sc)
        n_sc[...] = jnp.zeros_like(n_sc)
        m_sc[...] = jnp.zeros_like(m_sc)

    z = _softcap(gt_ref[...] + bias_ref[...])
    li_r = z[:heads]
    lf_r = _log_sigmoid(z[heads:])
    row = lax.broadcasted_iota(jnp.int32, (L, L), 0)
    col = lax.broadcasted_iota(jnp.int32, (L, L), 1)
    causal = col <= row
    upper = (row <= col).astype(BF16)
    eye = (row == col).astype(BF16)
    b_r = _dot_exact(lf_r, upper)
    cols = _transpose_exact(eye, jnp.concatenate([li_r, b_r], axis=0))

    for h in range(heads):
        q = q_ref[:, h * dk:(h + 1) * dk]
        k = k_ref[:, h * dk:(h + 1) * dk]
        v = v_ref[:, h * dv:(h + 1) * dv]
        li_c = cols[:, h:h + 1]
        b_c = cols[:, heads + h:heads + h + 1]
        m_prev = m_sc[h:h + 1, 0:1]
        n_prev = n_sc[h:h + 1, :]
        c_prev = c_sc[h]

        logw = jnp.where(causal, b_c - b_r[h:h + 1, :] + li_r[h:h + 1, :], -jnp.inf)
        log_inter = b_c + m_prev
        m_t = jnp.maximum(log_inter, jnp.max(logw, axis=1, keepdims=True))
        s = lax.dot_general(q, k, NT, preferred_element_type=F32) * scale * jnp.exp(logw - m_t)
        w_inter = jnp.exp(log_inter - m_t)
        num = (jnp.dot(s.astype(BF16), v, preferred_element_type=F32)
               + w_inter * jnp.dot(q, c_prev.astype(BF16), preferred_element_type=F32))
        qn = jnp.sum(q.astype(F32) * n_prev, axis=1, keepdims=True)
        den = jnp.sum(s, axis=1, keepdims=True) + w_inter * qn
        hh = num / jnp.maximum(jnp.abs(den), jnp.exp(-m_t))
        hh = hh * lax.rsqrt(jnp.mean(hh * hh, axis=1, keepdims=True) + EPS)
        og = o_ref[:, h * dv:(h + 1) * dv].astype(F32)
        hc_ref[:, h * dv:(h + 1) * dv] = (
            hh * gh_ref[:, h * dv:(h + 1) * dv] * jax.nn.sigmoid(og)).astype(hc_ref.dtype)

        m_end = m_t[L - 1:L, :]
        b_last = b_c[L - 1:L, :]
        w_end = jnp.exp(b_last - b_c + li_c - m_end)
        decay = jnp.exp(b_last + m_prev - m_end)
        kw = k.astype(F32) * (scale * w_end)
        c_sc[h] = decay * c_prev + lax.dot_general(kw.astype(BF16), v, TN,
                                                   preferred_element_type=F32)
        n_sc[h:h + 1, :] = decay * n_prev + jnp.sum(kw, axis=0, keepdims=True)
        m_sc[h:h + 1, :] = jnp.broadcast_to(m_end, (1, m_sc.shape[1]))

    @pl.when(c == pl.num_programs(1) - 1)
    def _():
        c_out[0] = c_sc[...]
        n_out[0] = n_sc[...]
        m_out[0] = m_sc[...]


def mlstm_prompt(proj, gates, bias, g_head, hc_buf, batch, seq, heads, dk, dv, chunk):
    hk, hv = heads * dk, heads * dv
    nc = seq // chunk
    row = lambda b, c: b * nc + c
    return pl.pallas_call(
        functools.partial(_mlstm_prompt_kernel, heads=heads, dk=dk, dv=dv),
        grid=(batch, nc),
        in_specs=[pl.BlockSpec((chunk, hk), lambda b, c: (row(b, c), 0)),
                  pl.BlockSpec((chunk, hk), lambda b, c: (row(b, c), 1)),
                  pl.BlockSpec((chunk, hv), lambda b, c: (row(b, c), 2 * hk // hv)),
                  pl.BlockSpec((chunk, hv), lambda b, c: (row(b, c), 2 * hk // hv + 1)),
                  pl.BlockSpec((2 * heads, chunk), lambda b, c: (0, row(b, c))),
                  pl.BlockSpec((2 * heads, 1), lambda b, c: (0, 0)),
                  pl.BlockSpec((1, hv), lambda b, c: (0, 0)),
                  pl.BlockSpec(memory_space=pl.ANY)],
        out_specs=[pl.BlockSpec((chunk, hv), lambda b, c: (row(b, c), 0)),
                   pl.BlockSpec((1, heads, dk, dv), lambda b, c: (b, 0, 0, 0)),
                   pl.BlockSpec((1, heads, dk), lambda b, c: (b, 0, 0)),
                   pl.BlockSpec((1, heads, LANES_V7X), lambda b, c: (b, 0, 0))],
        out_shape=[jax.ShapeDtypeStruct(hc_buf.shape, hc_buf.dtype),
                   jax.ShapeDtypeStruct((batch, heads, dk, dv), F32),
                   jax.ShapeDtypeStruct((batch, heads, dk), F32),
                   jax.ShapeDtypeStruct((batch, heads, LANES_V7X), F32)],
        scratch_shapes=[pltpu.VMEM((heads, dk, dv), F32),
                        pltpu.VMEM((heads, dk), F32),
                        pltpu.VMEM((heads, LANES_V7X), F32)],
        input_output_aliases={7: 0},
        compiler_params=_params("arbitrary", "arbitrary"),
        name="mlstm_prompt",
    )(proj, proj, proj, proj, gates, bias, g_head, hc_buf)


def _mlstm_sample_kernel(q_ref, k_ref, v_ref, o_ref, gt_ref, bias_ref, gh_ref, m0_ref,
                         n0_ref, c0_ref, hc_prev_ref, hc_ref, c_out, n_out, m_out,
                         *, heads, dk, dv, ts):
    del hc_prev_ref
    R = q_ref.shape[0]
    G = R // ts
    scale = dk ** -0.5

    z = _softcap(gt_ref[0] + bias_ref[...])
    li_r = z[:heads]
    lf_r = _log_sigmoid(z[heads:])
    row = lax.broadcasted_iota(jnp.int32, (R, R), 0)
    col = lax.broadcasted_iota(jnp.int32, (R, R), 1)
    seq_id = lambda i: sum((i >= g * ts).astype(jnp.int32) for g in range(1, G))
    same = seq_id(row) == seq_id(col)
    causal = same & (col <= row)
    upper = (same & (row <= col)).astype(BF16)
    eye = (row == col).astype(BF16)
    b_r = _dot_exact(lf_r, upper)
    cols = _transpose_exact(eye, jnp.concatenate([li_r, b_r], axis=0))
    seq_of_row = seq_id(lax.broadcasted_iota(jnp.int32, (R, 1), 0))
    m0 = m0_ref[0]

    for h in range(heads):
        q = q_ref[:, h * dk:(h + 1) * dk]
        k = k_ref[:, h * dk:(h + 1) * dk]
        v = v_ref[:, h * dv:(h + 1) * dv]
        li_c = cols[:, h:h + 1]
        b_c = cols[:, heads + h:heads + h + 1]
        m_prev = m0[:, h:h + 1]

        logw = jnp.where(causal, b_c - b_r[h:h + 1, :] + li_r[h:h + 1, :], -jnp.inf)
        log_inter = b_c + m_prev
        m_t = jnp.maximum(log_inter, jnp.max(logw, axis=1, keepdims=True))
        s = lax.dot_general(q, k, NT, preferred_element_type=F32) * scale * jnp.exp(logw - m_t)
        w_inter = jnp.exp(log_inter - m_t)

        qc = jnp.zeros((R, dv), F32)
        qn = jnp.zeros((R, 1), F32)
        qf = q.astype(F32)
        for g in range(G):
            mine = seq_of_row == g
            qc = jnp.where(mine, jnp.dot(q, c0_ref[g, h].astype(BF16),
                                         preferred_element_type=F32), qc)
            qn = jnp.where(mine, jnp.sum(qf * n0_ref[g, h:h + 1, :], axis=1, keepdims=True), qn)
        num = jnp.dot(s.astype(BF16), v, preferred_element_type=F32) + w_inter * qc
        den = jnp.sum(s, axis=1, keepdims=True) + w_inter * qn
        hh = num / jnp.maximum(jnp.abs(den), jnp.exp(-m_t))
        hh = hh * lax.rsqrt(jnp.mean(hh * hh, axis=1, keepdims=True) + EPS)
        og = o_ref[:, h * dv:(h + 1) * dv].astype(F32)
        hc_ref[:, h * dv:(h + 1) * dv] = (
            hh * gh_ref[:, h * dv:(h + 1) * dv] * jax.nn.sigmoid(og)).astype(hc_ref.dtype)

        kf = k.astype(F32)
        for g in range(G):
            last = (g + 1) * ts - 1
            m_end = m_t[last:last + 1, :]
            b_last = b_c[last:last + 1, :]
            mine = seq_of_row == g
            w_end = jnp.where(mine, jnp.exp(b_last - b_c + li_c - m_end), 0.0)
            decay = jnp.exp(b_last + m_prev[last:last + 1, :] - m_end)
            kw = kf * (scale * w_end)
            c_out[g, h] = decay * c0_ref[g, h] + lax.dot_general(
                kw.astype(BF16), v, TN, preferred_element_type=F32)
            n_out[g, h:h + 1, :] = decay * n0_ref[g, h:h + 1, :] + jnp.sum(kw, axis=0, keepdims=True)
            m_out[g, h:h + 1, :] = jnp.broadcast_to(m_end, (1, m_out.shape[2]))


def mlstm_sample(proj, gates_blk, bias, g_head, m0_blk, n0, c0, hc_buf, row_start,
                 heads, dk, dv, ts, group):
    hk, hv = heads * dk, heads * dv
    rows = group * ts
    nb = c0.shape[0]
    steps = nb // group
    assert row_start % rows == 0
    first = row_start // rows
    return pl.pallas_call(
        functools.partial(_mlstm_sample_kernel, heads=heads, dk=dk, dv=dv, ts=ts),
        grid=(steps,),
        in_specs=[pl.BlockSpec((rows, hk), lambda i: (i + first, 0)),
                  pl.BlockSpec((rows, hk), lambda i: (i + first, 1)),
                  pl.BlockSpec((rows, hv), lambda i: (i + first, 2 * hk // hv)),
                  pl.BlockSpec((rows, hv), lambda i: (i + first, 2 * hk // hv + 1)),
                  pl.BlockSpec((1, 2 * heads, rows), lambda i: (i, 0, 0)),
                  pl.BlockSpec((2 * heads, 1), lambda i: (0, 0)),
                  pl.BlockSpec((1, hv), lambda i: (0, 0)),
                  pl.BlockSpec((1, rows, heads), lambda i: (i, 0, 0)),
                  pl.BlockSpec((group, heads, dk), lambda i: (i, 0, 0)),
                  pl.BlockSpec((group, heads, dk, dv), lambda i: (i, 0, 0, 0)),
                  pl.BlockSpec(memory_space=pl.ANY)],
        out_specs=[pl.BlockSpec((rows, hv), lambda i: (i + first, 0)),
                   pl.BlockSpec((group, heads, dk, dv), lambda i: (i, 0, 0, 0)),
                   pl.BlockSpec((group, heads, dk), lambda i: (i, 0, 0)),
                   pl.BlockSpec((group, heads, LANES_V7X), lambda i: (i, 0, 0))],
        out_shape=[jax.ShapeDtypeStruct(hc_buf.shape, hc_buf.dtype),
                   jax.ShapeDtypeStruct((nb, heads, dk, dv), F32),
                   jax.ShapeDtypeStruct((nb, heads, dk), F32),
                   jax.ShapeDtypeStruct((nb, heads, LANES_V7X), F32)],
        input_output_aliases={10: 0},
        compiler_params=_params("arbitrary"),
        name="mlstm_sample",
    )(proj, proj, proj, proj, gates_blk, bias, g_head, m0_blk, n0, c0, hc_buf)


def _online_softmax(s, m_prev, c_exp):
    m_new = jnp.maximum(m_prev, jnp.max(s, axis=1, keepdims=True))
    alpha = jnp.exp2((m_prev - m_new) * c_exp)
    p = jnp.exp2((s - m_new) * c_exp)
    return m_new, alpha, p


def _attn_prompt_kernel(q_ref, kn_ref, kr_ref, v_ref, o_prev_ref, o_ref, kf_sc, vx_sc, m_sc,
                        acc_sc, *, c_exp, nope, vdim, parts):
    del o_prev_ref
    qi = pl.program_id(2)
    tq = q_ref.shape[0]
    tp = tq // parts

    @pl.when(qi == 0)
    def _():
        kf_sc[:, :nope] = kn_ref[...]
        kf_sc[:, nope:] = kr_ref[...]
        vx_sc[:, :vdim] = v_ref[...]
        lane = lax.broadcasted_iota(jnp.int32, (vx_sc.shape[0], vx_sc.shape[1] - vdim), 1)
        vx_sc[:, vdim:] = (lane == 0).astype(BF16)

    m_sc[...] = jnp.full_like(m_sc, -jnp.inf)
    acc_sc[...] = jnp.zeros_like(acc_sc)

    def update(part, k, vx, row_offset):
        s = lax.dot_general(q_ref[part * tp:(part + 1) * tp, :], k, NT, preferred_element_type=F32)
        if row_offset is not None:
            row = lax.broadcasted_iota(jnp.int32, s.shape, 0) + row_offset
            col = lax.broadcasted_iota(jnp.int32, s.shape, 1)
            s = jnp.where(col <= row, s, -jnp.inf)
        m_new, alpha, p = _online_softmax(s, m_sc[part], c_exp)
        acc_sc[part] = alpha * acc_sc[part] + jnp.dot(p.astype(BF16), vx,
                                                      preferred_element_type=F32)
        m_sc[part] = m_new

    def body(ki, carry):
        start = pl.multiple_of(ki * tq, tq)
        k = kf_sc[pl.ds(start, tq), :]
        vx = vx_sc[pl.ds(start, tq), :]
        for part in range(parts):
            update(part, k, vx, None)
        return carry

    lax.fori_loop(0, qi, body, 0)
    start = pl.multiple_of(qi * tq, tq)
    for part in range(parts):
        n_keys = (part + 1) * tp
        update(part, kf_sc[pl.ds(start, n_keys), :], vx_sc[pl.ds(start, n_keys), :], part * tp)
    for part in range(parts):
        acc = acc_sc[part]
        o_ref[part * tp:(part + 1) * tp, :] = (acc[:, :vdim] / acc[:, vdim:vdim + 1]).astype(o_ref.dtype)


def attn_prompt(q, kn, kr, v, o_buf, batch, seq, heads, nope, vdim, scale, tq, parts):
    hw = q.shape[1] // heads
    nq = seq // tq
    return pl.pallas_call(
        functools.partial(_attn_prompt_kernel, c_exp=scale * LOG2E, nope=nope, vdim=vdim,
                          parts=parts),
        grid=(batch, heads, nq),
        in_specs=[pl.BlockSpec((tq, hw), lambda b, h, i: (b * nq + i, h)),
                  pl.BlockSpec((seq, nope), lambda b, h, i: (b, h)),
                  pl.BlockSpec((seq, hw - nope), lambda b, h, i: (b, 0)),
                  pl.BlockSpec((seq, vdim), lambda b, h, i: (b, h)),
                  pl.BlockSpec(memory_space=pl.ANY)],
        out_specs=pl.BlockSpec((tq, vdim), lambda b, h, i: (b * nq + i, h)),
        out_shape=jax.ShapeDtypeStruct(o_buf.shape, o_buf.dtype),
        input_output_aliases={4: 0},
        scratch_shapes=[pltpu.VMEM((seq, hw), BF16),
                        pltpu.VMEM((seq, 2 * vdim), BF16),
                        pltpu.VMEM((parts, tq // parts, 1), F32),
                        pltpu.VMEM((parts, tq // parts, 2 * vdim), F32)],
        compiler_params=_params("arbitrary", "arbitrary", "arbitrary"),
        name="attn_prompt",
    )(q, kn, kr, v, o_buf)


def _qlat_kernel(q_ref, w_ref, o_ref, *, nope):
    q = q_ref[...]
    lat = lax.dot_general(q[:, :nope], w_ref[...].astype(BF16), NT, preferred_element_type=F32)
    o_ref[0] = jnp.concatenate([lat.astype(BF16), q[:, nope:]], axis=1)


def q_latent(q, w_uk2, heads, nope, row_start, n_rows):
    m = n_rows
    assert row_start % m == 0
    first = row_start // m
    hw = q.shape[1] // heads
    rank = w_uk2.shape[0]
    return pl.pallas_call(
        functools.partial(_qlat_kernel, nope=nope),
        grid=(heads,),
        in_specs=[pl.BlockSpec((m, hw), lambda h: (first, h)),
                  pl.BlockSpec((rank, nope), lambda h: (0, h))],
        out_specs=pl.BlockSpec((1, m, rank + hw - nope), lambda h: (h, 0, 0)),
        out_shape=jax.ShapeDtypeStruct((heads, m, rank + hw - nope), BF16),
        compiler_params=_params("arbitrary"),
        name="q_latent",
    )(q, w_uk2)


def _decode_kernel(pt_ref, q_ref, new_ref, ck_hbm, kr_hbm, o_ref, ck_buf, kr_buf, sems,
                   kc_sc, krt_sc, m_sc, l_sc, acc_sc, *, pages, c_exp, rank, rope, ts):
    b, j = pl.program_id(0), pl.program_id(1)
    nb, nj = pl.num_programs(0), pl.num_programs(1)
    psz = ck_buf.shape[2]
    step = b * nj + j
    slot = lax.rem(step, 2)

    def page_copies(bb, jj, sl):
        copies = []
        for p in range(pages):
            page = pt_ref[bb, jj * pages + p]
            copies.append(pltpu.make_async_copy(ck_hbm.at[page], ck_buf.at[sl, p], sems.at[0, sl]))
            copies.append(pltpu.make_async_copy(kr_hbm.at[page], kr_buf.at[sl, p], sems.at[1, sl]))
        return copies

    def start_all(copies):
        for i, c in enumerate(copies):
            c.start(priority=(i // 2) % 2)

    @pl.when(step == 0)
    def _():
        start_all(page_copies(b, j, slot))

    @pl.when(step + 1 < nb * nj)
    def _():
        last = j == nj - 1
        start_all(page_copies(jnp.where(last, b + 1, b), jnp.where(last, 0, j + 1), 1 - slot))

    for c in page_copies(b, j, slot):
        c.wait()

    @pl.when(j == 0)
    def _():
        m_sc[...] = jnp.full_like(m_sc, -jnp.inf)
        l_sc[...] = jnp.zeros_like(l_sc)
        acc_sc[...] = jnp.zeros_like(acc_sc)

    for p in range(pages):
        kc_sc[p * psz:(p + 1) * psz, :] = ck_buf[slot, p].astype(BF16)
        krt_sc[:, p * psz:(p + 1) * psz] = kr_buf[slot, p].astype(BF16)

    q = q_ref[0]
    s = (lax.dot_general(q[:, :rank], kc_sc[...], NT, preferred_element_type=F32)
         + jnp.dot(q[:, rank:rank + rope], krt_sc[...], preferred_element_type=F32))
    m, alpha, p = _online_softmax(s, m_sc[...], c_exp)
    l = alpha * l_sc[...] + jnp.sum(p, axis=1, keepdims=True)
    acc = alpha * acc_sc[...] + jnp.dot(p.astype(BF16), kc_sc[...], preferred_element_type=F32)
    m_sc[...] = m
    l_sc[...] = l
    acc_sc[...] = acc

    @pl.when(j == pl.num_programs(1) - 1)
    def _():
        new = new_ref[0]
        s2 = lax.dot_general(q, new, NT, preferred_element_type=F32)
        qrow = lax.broadcasted_iota(jnp.int32, s2.shape, 0)
        heads = q.shape[0] // ts
        tok = sum((qrow >= t * heads).astype(jnp.int32) for t in range(1, ts))
        key = lax.broadcasted_iota(jnp.int32, s2.shape, 1)
        s2 = jnp.where(key <= tok, s2, -jnp.inf)
        _, alpha, p = _online_softmax(s2, m, c_exp)
        l2 = alpha * l + jnp.sum(p, axis=1, keepdims=True)
        acc2 = alpha * acc + jnp.dot(p.astype(BF16), new[:, :rank], preferred_element_type=F32)
        o_ref[0] = acc2 / l2


def attn_decode(q_full, new_keys, cache_ckv, cache_krope_t, page_table, scale, ts, pages):
    nb, rows, width = q_full.shape
    _, psz, rank = cache_ckv.shape
    rope = cache_krope_t.shape[1]
    n_pages = page_table.shape[1]
    nj = n_pages // pages

    grid_spec = pltpu.PrefetchScalarGridSpec(
        num_scalar_prefetch=1,
        grid=(nb, nj),
        in_specs=[pl.BlockSpec((1, rows, width), lambda b, j, pt: (b, 0, 0)),
                  pl.BlockSpec((1, new_keys.shape[1], width), lambda b, j, pt: (b, 0, 0)),
                  pl.BlockSpec(memory_space=pl.ANY),
                  pl.BlockSpec(memory_space=pl.ANY)],
        out_specs=pl.BlockSpec((1, rows, rank), lambda b, j, pt: (b, 0, 0)),
        scratch_shapes=[pltpu.VMEM((2, pages, psz, rank), cache_ckv.dtype),
                        pltpu.VMEM((2, pages, rope, psz), cache_krope_t.dtype),
                        pltpu.SemaphoreType.DMA((2, 2)),
                        pltpu.VMEM((pages * psz, rank), BF16),
                        pltpu.VMEM((rope, pages * psz), BF16),
                        pltpu.VMEM((rows, 1), F32),
                        pltpu.VMEM((rows, 1), F32),
                        pltpu.VMEM((rows, rank), F32)],
    )
    return pl.pallas_call(
        functools.partial(_decode_kernel, pages=pages, c_exp=scale * LOG2E, rank=rank,
                          rope=rope, ts=ts),
        grid_spec=grid_spec,
        out_shape=jax.ShapeDtypeStruct((nb, rows, rank), F32),
        compiler_params=_params("arbitrary", "arbitrary"),
        name="attn_decode",
    )(page_table, q_full, new_keys, cache_ckv, cache_krope_t)


def _oup_kernel(o_ref, w_ref, prev_ref, out_ref):
    del prev_ref
    out_ref[...] = jnp.dot(o_ref[0], w_ref[...].astype(BF16),
                           preferred_element_type=F32).astype(out_ref.dtype)


def o_up(o_lat, w_uv2, vdim, o_buf, row_start):
    heads, m, rank = o_lat.shape
    assert row_start % m == 0
    first = row_start // m
    return pl.pallas_call(
        _oup_kernel,
        grid=(heads,),
        in_specs=[pl.BlockSpec((1, m, rank), lambda h: (h, 0, 0)),
                  pl.BlockSpec((rank, vdim), lambda h: (0, h)),
                  pl.BlockSpec(memory_space=pl.ANY)],
        out_specs=pl.BlockSpec((m, vdim), lambda h: (first, h)),
        out_shape=jax.ShapeDtypeStruct(o_buf.shape, o_buf.dtype),
        input_output_aliases={2: 0},
        compiler_params=_params("arbitrary"),
        name="o_up",
    )(o_lat, w_uv2, o_buf)


def _rope_tables(pos, rope_dim, nope_pad, width):
    half = rope_dim // 2
    freqs = ROPE_THETA ** (-jnp.arange(half, dtype=F32) / half)
    ang = pos.astype(F32)[:, None] * freqs[None, :]
    cos, sin = jnp.cos(ang), jnp.sin(ang)
    n = pos.shape[0]
    zeros = lambda w: jnp.zeros((n, w), F32)
    tail = width - nope_pad - rope_dim
    ta = jnp.concatenate([jnp.ones((n, nope_pad), F32), cos, cos, zeros(tail)], axis=1)
    tb = jnp.concatenate([zeros(nope_pad + half), sin, zeros(tail)], axis=1)
    tc = jnp.concatenate([zeros(nope_pad), -sin, zeros(half + tail)], axis=1)
    return ta, tb, tc


def kernel(x_prompt, x_sample, cache_ckv, cache_krope, page_table, state_C, state_n, state_m,
           g_mix, g_ffn, w_ffn_in, w_ffn_out, w_a_in, b_a_gates, g_a_head, w_a_out,
           g_kv_in, w_kv_down, g_kv_latent, w_uk, w_uv, w_q_down, g_q_latent, w_q_up,
           w_b_out, g_final):
    bp, tp, d = x_prompt.shape
    bs, ts, _ = x_sample.shape
    mp, ms = bp * tp, bs * ts
    n_a = state_C.shape[0]
    n_layers = g_mix.shape[0]
    assert n_a == 1 and n_layers == 2, "one mLSTM layer followed by one MLA layer"
    _, _, heads_a, dk, dv = state_C.shape
    hk, hv = heads_a * dk, heads_a * dv
    rank, heads_b, nope = w_uk.shape
    vdim = w_uv.shape[2]
    rope = cache_krope.shape[2]
    psz = cache_ckv.shape[1]
    past = page_table.shape[1] * psz
    hw = nope + 2 * rope
    assert nope % LANES_V7X == 0 and hw % LANES_V7X == 0 and (2 * rope) % LANES_V7X == 0
    attn_scale = (nope + rope) ** -0.5

    x = jnp.concatenate([x_prompt.reshape(mp, d), x_sample.reshape(ms, d)], axis=0)
    pos = jnp.concatenate([jnp.tile(jnp.arange(tp), bp), jnp.tile(past + jnp.arange(ts), bs)])
    tabs = _rope_tables(pos, rope, nope, hw)

    (xn,) = rmsnorm_rows(x, g_mix[0:1])
    w_in_t = jnp.swapaxes(w_a_in[0], 0, 1)
    proj = matmul(xn, w_in_t, n_out=2 * hk + 2 * hv, out_dtype=BF16, w_t=True,
                  name="mlstm_in_proj")
    gates = gates_t(xn, w_in_t[2 * hk + 2 * hv:])
    bias = b_a_gates[0].reshape(2 * heads_a, 1)
    g_head = g_a_head[0].reshape(1, hv)

    hc = jnp.zeros((mp + ms, hv), BF16)
    hc, c_p, n_p, m_p = mlstm_prompt(proj, gates, bias, g_head, hc, bp, tp, heads_a, dk, dv,
                                     chunk=min(256, tp))
    group = 4
    rows = group * ts
    gates_blk = gates[:, mp:].reshape(2 * heads_a, bs // group, rows).transpose(1, 0, 2)
    m0_blk = jnp.repeat(state_m[0], ts, axis=0).reshape(bs // group, rows, heads_a)
    hc, c_s, n_s, m_s = mlstm_sample(proj, gates_blk, bias, g_head, m0_blk, state_n[0],
                                     state_C[0], hc, mp, heads_a, dk, dv, ts, group)
    x = matmul(hc, w_a_out[0], res=x, name="mlstm_out_proj")

    (xn,) = rmsnorm_rows(x, g_ffn[0:1])
    x = matmul(swiglu_in(xn, w_ffn_in, 0), w_ffn_out, layer=0, res=x, name="ffn_out")

    xn_kv, xn_q = rmsnorm_rows(x, jnp.stack([g_kv_in, g_mix[1]]))
    w_kv_pad = jnp.pad(w_kv_down, ((0, 0), (0, rope)))
    g_lat = g_kv_latent.reshape(1, rank)
    ckv_p, ckvb_p, kr_p, krb_p = kv_down(xn_kv, w_kv_pad, g_lat, tabs, rank, rope, 0, mp)
    ckv_s, ckvb_s, kr_s, krb_s = kv_down(xn_kv, w_kv_pad, g_lat, tabs, rank, rope, mp, ms)

    xq = q_down(xn_q, w_q_down[0], g_q_latent[0].reshape(1, rank))
    w_qup_pad = jnp.pad(w_q_up[0].reshape(rank, heads_b, nope + rope),
                        ((0, 0), (0, 0), (0, rope))).reshape(rank, heads_b * hw)
    q = q_up(xq, w_qup_pad, tabs, hw)

    w_uk2 = w_uk.reshape(rank, heads_b * nope)
    w_uv2 = w_uv.reshape(rank, heads_b * vdim)
    kn = matmul(ckvb_p, w_uk2, out_dtype=BF16, name="k_up")
    v = matmul(ckvb_p, w_uv2, out_dtype=BF16, name="v_up")
    tq = min(2048, tp)
    o = jnp.zeros((mp + ms, heads_b * vdim), BF16)
    o = attn_prompt(q, kn, krb_p, v, o, bp, tp, heads_b, nope, vdim, attn_scale,
                    tq=tq, parts=max(1, tq // 256))

    q_lat = q_latent(q, w_uk2, heads_b, nope, mp, ms)
    width = rank + 2 * rope
    q_full = q_lat.reshape(heads_b, bs, ts, width).transpose(1, 2, 0, 3).reshape(bs, ts * heads_b, width)
    new_keys = jnp.concatenate([ckvb_s, krb_s], axis=1).reshape(bs, ts, width)
    new_keys = jnp.pad(new_keys, ((0, 0), (0, BF16_SUBLANES_V7X - ts), (0, 0)))
    cache_krope_t = jnp.swapaxes(cache_krope, 1, 2)
    o_lat = attn_decode(q_full, new_keys, cache_ckv, cache_krope_t, page_table, attn_scale, ts,
                        pages=min(16, page_table.shape[1]))
    o_lat = o_lat.reshape(bs, ts, heads_b, rank).transpose(2, 0, 1, 3).reshape(heads_b, ms, rank)
    o = o_up(o_lat.astype(BF16), w_uv2, vdim, o, mp)

    x = matmul(o, w_b_out[0], res=x, name="mla_out_proj")

    (xn,) = rmsnorm_rows(x, g_ffn[1:2])
    x = matmul(swiglu_in(xn, w_ffn_in, 1), w_ffn_out, layer=1, res=x, name="ffn_out")
    g_fin = g_final.reshape(1, d)
    (y_p,) = rmsnorm_rows(x, g_fin, out_dtype=F32, row_start=0, n_rows=mp)
    (y_s,) = rmsnorm_rows(x, g_fin, out_dtype=F32, row_start=mp, n_rows=ms)

    return (y_p.reshape(bp, tp, d), y_s.reshape(bs, ts, d),
            c_p[None], n_p[None], m_p[None, :, :, 0],
            ckv_p.reshape(bp, tp, rank), kr_p.reshape(bp, tp, rope),
            c_s[None], n_s[None], m_s[None, :, :, 0],
            ckv_s.reshape(bs, ts, rank), kr_s.reshape(bs, ts, rope))
```

```python
import functools

import jax
import jax.numpy as jnp
from jax import lax
from jax.experimental import pallas as pl
from jax.experimental.pallas import tpu as pltpu

F32 = jnp.float32
BF16 = jnp.bfloat16

EPS = 1e-6
GATE_CAP = 15.0
ROPE_THETA = 10000.0
LOG2E = 1.4426950408889634

LANES_V7X = 128
BF16_SUBLANES_V7X = 16
VMEM_BYTES_V7X = 64 * 1024 * 1024
VMEM_BUDGET = VMEM_BYTES_V7X * 5 // 8
VMEM_LIMIT = VMEM_BYTES_V7X * 7 // 8

NT = (((1,), (1,)), ((), ()))
TN = (((0,), (0,)), ((), ()))


def _params(*sem):
    return pltpu.CompilerParams(dimension_semantics=sem, vmem_limit_bytes=VMEM_LIMIT)


def _largest_divisor(n, cap, mult):
    best = None
    for d in range(mult, min(n, cap) + 1, mult):
        if n % d == 0:
            best = d
    return n if best is None else best


def _row_tile(m, cap=1088):
    return _largest_divisor(m, cap, BF16_SUBLANES_V7X)


def _rms_kernel(x_ref, g_ref, *o_refs):
    x = x_ref[...]
    y = x * lax.rsqrt(jnp.mean(x * x, axis=-1, keepdims=True) + EPS)
    for i, o_ref in enumerate(o_refs):
        o_ref[...] = (y * g_ref[i:i + 1, :]).astype(o_ref.dtype)


def rmsnorm_rows(x, gains, out_dtype=BF16, row_start=0, n_rows=None):
    d = x.shape[1]
    m = x.shape[0] if n_rows is None else n_rows
    n = gains.shape[0]
    tm = _largest_divisor(m, 512, BF16_SUBLANES_V7X)
    assert row_start % tm == 0
    first = row_start // tm
    outs = pl.pallas_call(
        _rms_kernel,
        grid=(m // tm,),
        in_specs=[pl.BlockSpec((tm, d), lambda i: (i + first, 0)),
                  pl.BlockSpec((n, d), lambda i: (0, 0))],
        out_specs=[pl.BlockSpec((tm, d), lambda i: (i, 0))] * n,
        out_shape=[jax.ShapeDtypeStruct((m, d), out_dtype)] * n,
        compiler_params=_params("arbitrary"),
        name="rmsnorm_rows",
    )(x, gains)
    return outs


def _mm_tiles(m, k, n, out_bytes, n_w=1, has_res=False, w_buffers=2):
    best = None
    for tn in (512, 256, 128):
        if n % tn:
            continue
        for tm in range(BF16_SUBLANES_V7X, min(m, 1088) + 1, BF16_SUBLANES_V7X):
            if m % tm:
                continue
            need = (2 * tm * k * 2 + n_w * (w_buffers * k * tn * 4 + k * tn * 2)
                    + 2 * tm * tn * out_bytes + (2 * tm * tn * 4 if has_res else 0))
            if need <= VMEM_BUDGET and (best is None or tm * tn > best[0] * best[1]):
                best = (tm, tn)
    assert best is not None, (m, k, n)
    return best


def _mm_kernel(a_ref, w_ref, *rest, has_res, w_t):
    if has_res:
        r_ref, o_ref, w_sc = rest
    else:
        o_ref, w_sc = rest

    @pl.when(pl.program_id(1) == 0)
    def _():
        w_sc[...] = w_ref[...].astype(BF16)

    if w_t:
        acc = lax.dot_general(a_ref[...], w_sc[...], NT, preferred_element_type=F32)
    else:
        acc = jnp.dot(a_ref[...], w_sc[...], preferred_element_type=F32)
    if has_res:
        acc = acc + r_ref[...]
    o_ref[...] = acc.astype(o_ref.dtype)


def _layer_spec(block, index_map, layer, buffers=2):
    mode = {} if buffers == 2 else {"pipeline_mode": pl.Buffered(buffers)}
    if layer is None:
        return pl.BlockSpec(block, index_map, **mode)
    return pl.BlockSpec((None,) + block, lambda *g: (layer,) + index_map(*g), **mode)


def matmul(a, w, n_out=None, res=None, out_dtype=F32, w_t=False, layer=None, w_buffers=2,
           name="matmul"):
    m, k = a.shape
    w_shape = w.shape if layer is None else w.shape[1:]
    n = (w_shape[0] if w_t else w_shape[1]) if n_out is None else n_out
    tm, tn = _mm_tiles(m, k, n, jnp.dtype(out_dtype).itemsize, has_res=res is not None,
                       w_buffers=w_buffers)
    w_block = (tn, k) if w_t else (k, tn)
    w_map = (lambda j, i: (j, 0)) if w_t else (lambda j, i: (0, j))
    in_specs = [pl.BlockSpec((tm, k), lambda j, i: (i, 0)),
                _layer_spec(w_block, w_map, layer, w_buffers)]
    args = [a, w]
    if res is not None:
        in_specs.append(pl.BlockSpec((tm, tn), lambda j, i: (i, j)))
        args.append(res)
    return pl.pallas_call(
        functools.partial(_mm_kernel, has_res=res is not None, w_t=w_t),
        grid=(n // tn, m // tm),
        in_specs=in_specs,
        out_specs=pl.BlockSpec((tm, tn), lambda j, i: (i, j)),
        out_shape=jax.ShapeDtypeStruct((m, n), out_dtype),
        scratch_shapes=[pltpu.VMEM(w_block, BF16)],
        compiler_params=_params("arbitrary", "arbitrary"),
        name=name,
    )(*args)


def _swiglu_kernel(a_ref, wg_ref, wu_ref, o_ref, wg_sc, wu_sc):
    @pl.when(pl.program_id(1) == 0)
    def _():
        wg_sc[...] = wg_ref[...].astype(BF16)
        wu_sc[...] = wu_ref[...].astype(BF16)

    a = a_ref[...]
    g = jnp.dot(a, wg_sc[...], preferred_element_type=F32)
    u = jnp.dot(a, wu_sc[...], preferred_element_type=F32)
    o_ref[...] = (g * jax.nn.sigmoid(g) * u).astype(o_ref.dtype)


def swiglu_in(a, w_in, layer):
    m, k = a.shape
    f = w_in.shape[2] // 2
    tm, tf = _mm_tiles(m, k, f, 2, n_w=2)
    nf = f // tf
    return pl.pallas_call(
        _swiglu_kernel,
        grid=(nf, m // tm),
        in_specs=[pl.BlockSpec((tm, k), lambda j, i: (i, 0)),
                  _layer_spec((k, tf), lambda j, i: (0, j), layer),
                  _layer_spec((k, tf), lambda j, i: (0, j + nf), layer)],
        out_specs=pl.BlockSpec((tm, tf), lambda j, i: (i, j)),
        out_shape=jax.ShapeDtypeStruct((m, f), BF16),
        scratch_shapes=[pltpu.VMEM((k, tf), BF16), pltpu.VMEM((k, tf), BF16)],
        compiler_params=_params("arbitrary", "arbitrary"),
        name="swiglu_in",
    )(a, w_in, w_in)


def _gates_kernel(wt_ref, a_ref, o_ref):
    o_ref[...] = lax.dot_general(wt_ref[...].astype(BF16), a_ref[...], NT,
                                 preferred_element_type=F32)


def gates_t(a, w_gate_t):
    m, k = a.shape
    g = w_gate_t.shape[0]
    tm = _largest_divisor(m, 512, LANES_V7X)
    return pl.pallas_call(
        _gates_kernel,
        grid=(m // tm,),
        in_specs=[pl.BlockSpec((g, k), lambda i: (0, 0)),
                  pl.BlockSpec((tm, k), lambda i: (i, 0))],
        out_specs=pl.BlockSpec((g, tm), lambda i: (0, i)),
        out_shape=jax.ShapeDtypeStruct((g, m), F32),
        compiler_params=_params("arbitrary"),
        name="gates_t",
    )(w_gate_t, a)


def _rope_lanes(x, a, b, c):
    w = x.shape[1]
    return x * a + pltpu.roll(x, 32, 1) * b + pltpu.roll(x, w - 32, 1) * c


def _qdown_kernel(a_ref, w_ref, g_ref, o_ref, w_sc):
    @pl.when(pl.program_id(0) == 0)
    def _():
        w_sc[...] = w_ref[...].astype(BF16)

    y = jnp.dot(a_ref[...], w_sc[...], preferred_element_type=F32)
    y = y * lax.rsqrt(jnp.mean(y * y, axis=-1, keepdims=True) + EPS)
    o_ref[...] = (y * g_ref[...]).astype(o_ref.dtype)


def q_down(a, w, g):
    m, k = a.shape
    r = w.shape[1]
    tm = _row_tile(m)
    return pl.pallas_call(
        _qdown_kernel,
        grid=(m // tm,),
        in_specs=[pl.BlockSpec((tm, k), lambda i: (i, 0)),
                  pl.BlockSpec((k, r), lambda i: (0, 0)),
                  pl.BlockSpec((1, r), lambda i: (0, 0))],
        out_specs=pl.BlockSpec((tm, r), lambda i: (i, 0)),
        out_shape=jax.ShapeDtypeStruct((m, r), BF16),
        scratch_shapes=[pltpu.VMEM((k, r), BF16)],
        compiler_params=_params("arbitrary"),
        name="q_down",
    )(a, w, g)


def _qup_kernel(a_ref, w_ref, ta_ref, tb_ref, tc_ref, o_ref, w_sc, *, hw):
    @pl.when(pl.program_id(0) == 0)
    def _():
        w_sc[...] = w_ref[...].astype(BF16)

    a = a_ref[...]
    ta, tb, tc = ta_ref[...], tb_ref[...], tc_ref[...]
    for h in range(w_sc.shape[1] // hw):
        sl = slice(h * hw, (h + 1) * hw)
        y = jnp.dot(a, w_sc[:, sl], preferred_element_type=F32)
        o_ref[:, sl] = _rope_lanes(y, ta, tb, tc).astype(o_ref.dtype)


def q_up(a, w_pad, tabs, hw):
    m, k = a.shape
    n = w_pad.shape[1]
    tm = _row_tile(m, 544)
    tspec = pl.BlockSpec((tm, hw), lambda i: (i, 0))
    return pl.pallas_call(
        functools.partial(_qup_kernel, hw=hw),
        grid=(m // tm,),
        in_specs=[pl.BlockSpec((tm, k), lambda i: (i, 0)),
                  pl.BlockSpec((k, n), lambda i: (0, 0)),
                  tspec, tspec, tspec],
        out_specs=pl.BlockSpec((tm, n), lambda i: (i, 0)),
        out_shape=jax.ShapeDtypeStruct((m, n), BF16),
        scratch_shapes=[pltpu.VMEM((k, n), BF16)],
        compiler_params=_params("arbitrary"),
        name="q_up",
    )(a, w_pad, *tabs)


def _kvdown_kernel(a_ref, w_ref, g_ref, ta_ref, tb_ref, tc_ref,
                   ckv_ref, ckvb_ref, kr_ref, krb_ref, w_sc, *, rank):
    @pl.when(pl.program_id(0) == 0)
    def _():
        w_sc[...] = w_ref[...].astype(BF16)

    y = jnp.dot(a_ref[...], w_sc[...], preferred_element_type=F32)
    lat = y[:, :rank]
    lat = lat * lax.rsqrt(jnp.mean(lat * lat, axis=-1, keepdims=True) + EPS) * g_ref[...]
    ckv_ref[...] = lat
    ckvb_ref[...] = lat.astype(BF16)
    kr = _rope_lanes(y[:, rank:], ta_ref[...], tb_ref[...], tc_ref[...])
    kr_ref[...] = kr[:, :kr_ref.shape[1]]
    krb_ref[...] = kr.astype(BF16)


def kv_down(a, w_pad, g, tabs, rank, rope, row_start, n_rows):
    k = a.shape[1]
    m = n_rows
    n = w_pad.shape[1]
    rw = n - rank
    tm = _row_tile(m)
    assert row_start % tm == 0
    first = row_start // tm
    tspec = pl.BlockSpec((tm, rw), lambda i: (i + first, 1))
    return pl.pallas_call(
        functools.partial(_kvdown_kernel, rank=rank),
        grid=(m // tm,),
        in_specs=[pl.BlockSpec((tm, k), lambda i: (i + first, 0)),
                  pl.BlockSpec((k, n), lambda i: (0, 0)),
                  pl.BlockSpec((1, rank), lambda i: (0, 0)),
                  tspec, tspec, tspec],
        out_specs=[pl.BlockSpec((tm, rank), lambda i: (i, 0)),
                   pl.BlockSpec((tm, rank), lambda i: (i, 0)),
                   pl.BlockSpec((tm, rope), lambda i: (i, 0)),
                   pl.BlockSpec((tm, rw), lambda i: (i, 0))],
        out_shape=[jax.ShapeDtypeStruct((m, rank), F32),
                   jax.ShapeDtypeStruct((m, rank), BF16),
                   jax.ShapeDtypeStruct((m, rope), F32),
                   jax.ShapeDtypeStruct((m, rw), BF16)],
        scratch_shapes=[pltpu.VMEM((k, n), BF16)],
        compiler_params=_params("arbitrary"),
        name="kv_down",
    )(a, w_pad, g, *tabs)


def _split3(x):
    hi = x.astype(BF16)
    r = x - hi.astype(F32)
    mid = r.astype(BF16)
    lo = (r - mid.astype(F32)).astype(BF16)
    return hi, mid, lo


def _dot_exact(x, m01):
    return sum(jnp.dot(p, m01, preferred_element_type=F32) for p in _split3(x))


def _transpose_exact(eye, x):
    return sum(lax.dot_general(eye, p, NT, preferred_element_type=F32) for p in _split3(x))


def _log_sigmoid(x):
    return jnp.minimum(x, 0.0) - jnp.log1p(jnp.exp(-jnp.abs(x)))


def _softcap(z):
    return GATE_CAP * jnp.tanh(z / GATE_CAP)


def _mlstm_prompt_kernel(q_ref, k_ref, v_ref, o_ref, gt_ref, bias_ref, gh_ref, hc_prev_ref,
                         hc_ref, c_out, n_out, m_out, c_sc, n_sc, m_sc, *, heads, dk, dv):
    del hc_prev_ref
    c = pl.program_id(1)
    L = q_ref.shape[0]
    scale = dk ** -0.5

    @pl.when(c == 0)
    def _():
        c_sc[...] = jnp.zeros_like(c_sc)
        n_sc[...] = jnp.zeros_like(n_sc)
        m_sc[...] = jnp.zeros_like(m_sc)

    z = _softcap(gt_ref[...] + bias_ref[...])
    li_r = z[:heads]
    lf_r = _log_sigmoid(z[heads:])
    row = lax.broadcasted_iota(jnp.int32, (L, L), 0)
    col = lax.broadcasted_iota(jnp.int32, (L, L), 1)
    causal = col <= row
    upper = (row <= col).astype(BF16)
    eye = (row == col).astype(BF16)
    b_r = _dot_exact(lf_r, upper)
    cols = _transpose_exact(eye, jnp.concatenate([li_r, b_r], axis=0))

    for h in range(heads):
        q = q_ref[:, h * dk:(h + 1) * dk]
        k = k_ref[:, h * dk:(h + 1) * dk]
        v = v_ref[:, h * dv:(h + 1) * dv]
        li_c = cols[:, h:h + 1]
        b_c = cols[:, heads + h:heads + h + 1]
        m_prev = m_sc[h:h + 1, 0:1]
        n_prev = n_sc[h:h + 1, :]
        c_prev = c_sc[h]

        logw = jnp.where(causal, b_c - b_r[h:h + 1, :] + li_r[h:h + 1, :], -jnp.inf)
        log_inter = b_c + m_prev
        m_t = jnp.maximum(log_inter, jnp.max(logw, axis=1, keepdims=True))
        s = lax.dot_general(q, k, NT, preferred_element_type=F32) * scale * jnp.exp(logw - m_t)
        w_inter = jnp.exp(log_inter - m_t)
        num = (jnp.dot(s.astype(BF16), v, preferred_element_type=F32)
               + w_inter * jnp.dot(q, c_prev.astype(BF16), preferred_element_type=F32))
        qn = jnp.sum(q.astype(F32) * n_prev, axis=1, keepdims=True)
        den = jnp.sum(s, axis=1, keepdims=True) + w_inter * qn
        hh = num / jnp.maximum(jnp.abs(den), jnp.exp(-m_t))
        hh = hh * lax.rsqrt(jnp.mean(hh * hh, axis=1, keepdims=True) + EPS)
        og = o_ref[:, h * dv:(h + 1) * dv].astype(F32)
        hc_ref[:, h * dv:(h + 1) * dv] = (
            hh * gh_ref[:, h * dv:(h + 1) * dv] * jax.nn.sigmoid(og)).astype(hc_ref.dtype)

        m_end = m_t[L - 1:L, :]
        b_last = b_c[L - 1:L, :]
        w_end = jnp.exp(b_last - b_c + li_c - m_end)
        decay = jnp.exp(b_last + m_prev - m_end)
        kw = k.astype(F32) * (scale * w_end)
        c_sc[h] = decay * c_prev + lax.dot_general(kw.astype(BF16), v, TN,
                                                   preferred_element_type=F32)
        n_sc[h:h + 1, :] = decay * n_prev + jnp.sum(kw, axis=0, keepdims=True)
        m_sc[h:h + 1, :] = jnp.broadcast_to(m_end, (1, m_sc.shape[1]))

    @pl.when(c == pl.num_programs(1) - 1)
    def _():
        c_out[0] = c_sc[...]
        n_out[0] = n_sc[...]
        m_out[0] = m_sc[...]


def mlstm_prompt(proj, gates, bias, g_head, hc_buf, batch, seq, heads, dk, dv, chunk):
    hk, hv = heads * dk, heads * dv
    nc = seq // chunk
    row = lambda b, c: b * nc + c
    return pl.pallas_call(
        functools.partial(_mlstm_prompt_kernel, heads=heads, dk=dk, dv=dv),
        grid=(batch, nc),
        in_specs=[pl.BlockSpec((chunk, hk), lambda b, c: (row(b, c), 0)),
                  pl.BlockSpec((chunk, hk), lambda b, c: (row(b, c), 1)),
                  pl.BlockSpec((chunk, hv), lambda b, c: (row(b, c), 2 * hk // hv)),
                  pl.BlockSpec((chunk, hv), lambda b, c: (row(b, c), 2 * hk // hv + 1)),
                  pl.BlockSpec((2 * heads, chunk), lambda b, c: (0, row(b, c))),
                  pl.BlockSpec((2 * heads, 1), lambda b, c: (0, 0)),
                  pl.BlockSpec((1, hv), lambda b, c: (0, 0)),
                  pl.BlockSpec(memory_space=pl.ANY)],
        out_specs=[pl.BlockSpec((chunk, hv), lambda b, c: (row(b, c), 0)),
                   pl.BlockSpec((1, heads, dk, dv), lambda b, c: (b, 0, 0, 0)),
                   pl.BlockSpec((1, heads, dk), lambda b, c: (b, 0, 0)),
                   pl.BlockSpec((1, heads, LANES_V7X), lambda b, c: (b, 0, 0))],
        out_shape=[jax.ShapeDtypeStruct(hc_buf.shape, hc_buf.dtype),
                   jax.ShapeDtypeStruct((batch, heads, dk, dv), F32),
                   jax.ShapeDtypeStruct((batch, heads, dk), F32),
                   jax.ShapeDtypeStruct((batch, heads, LANES_V7X), F32)],
        scratch_shapes=[pltpu.VMEM((heads, dk, dv), F32),
                        pltpu.VMEM((heads, dk), F32),
                        pltpu.VMEM((heads, LANES_V7X), F32)],
        input_output_aliases={7: 0},
        compiler_params=_params("arbitrary", "arbitrary"),
        name="mlstm_prompt",
    )(proj, proj, proj, proj, gates, bias, g_head, hc_buf)


def _mlstm_sample_kernel(q_ref, k_ref, v_ref, o_ref, gt_ref, bias_ref, gh_ref, m0_ref,
                         n0_ref, c0_ref, hc_prev_ref, hc_ref, c_out, n_out, m_out,
                         *, heads, dk, dv, ts):
    del hc_prev_ref
    R = q_ref.shape[0]
    G = R // ts
    scale = dk ** -0.5

    z = _softcap(gt_ref[0] + bias_ref[...])
    li_r = z[:heads]
    lf_r = _log_sigmoid(z[heads:])
    row = lax.broadcasted_iota(jnp.int32, (R, R), 0)
    col = lax.broadcasted_iota(jnp.int32, (R, R), 1)
    seq_id = lambda i: sum((i >= g * ts).astype(jnp.int32) for g in range(1, G))
    same = seq_id(row) == seq_id(col)
    causal = same & (col <= row)
    upper = (same & (row <= col)).astype(BF16)
    eye = (row == col).astype(BF16)
    b_r = _dot_exact(lf_r, upper)
    cols = _transpose_exact(eye, jnp.concatenate([li_r, b_r], axis=0))
    seq_of_row = seq_id(lax.broadcasted_iota(jnp.int32, (R, 1), 0))
    m0 = m0_ref[0]

    for h in range(heads):
        q = q_ref[:, h * dk:(h + 1) * dk]
        k = k_ref[:, h * dk:(h + 1) * dk]
        v = v_ref[:, h * dv:(h + 1) * dv]
        li_c = cols[:, h:h + 1]
        b_c = cols[:, heads + h:heads + h + 1]
        m_prev = m0[:, h:h + 1]

        logw = jnp.where(causal, b_c - b_r[h:h + 1, :] + li_r[h:h + 1, :], -jnp.inf)
        log_inter = b_c + m_prev
        m_t = jnp.maximum(log_inter, jnp.max(logw, axis=1, keepdims=True))
        s = lax.dot_general(q, k, NT, preferred_element_type=F32) * scale * jnp.exp(logw - m_t)
        w_inter = jnp.exp(log_inter - m_t)

        qc = jnp.zeros((R, dv), F32)
        qn = jnp.zeros((R, 1), F32)
        qf = q.astype(F32)
        for g in range(G):
            mine = seq_of_row == g
            qc = jnp.where(mine, jnp.dot(q, c0_ref[g, h].astype(BF16),
                                         preferred_element_type=F32), qc)
            qn = jnp.where(mine, jnp.sum(qf * n0_ref[g, h:h + 1, :], axis=1, keepdims=True), qn)
        num = jnp.dot(s.astype(BF16), v, preferred_element_type=F32) + w_inter * qc
        den = jnp.sum(s, axis=1, keepdims=True) + w_inter * qn
        hh = num / jnp.maximum(jnp.abs(den), jnp.exp(-m_t))
        hh = hh * lax.rsqrt(jnp.mean(hh * hh, axis=1, keepdims=True) + EPS)
        og = o_ref[:, h * dv:(h + 1) * dv].astype(F32)
        hc_ref[:, h * dv:(h + 1) * dv] = (
            hh * gh_ref[:, h * dv:(h + 1) * dv] * jax.nn.sigmoid(og)).astype(hc_ref.dtype)

        kf = k.astype(F32)
        for g in range(G):
            last = (g + 1) * ts - 1
            m_end = m_t[last:last + 1, :]
            b_last = b_c[last:last + 1, :]
            mine = seq_of_row == g
            w_end = jnp.where(mine, jnp.exp(b_last - b_c + li_c - m_end), 0.0)
            decay = jnp.exp(b_last + m_prev[last:last + 1, :] - m_end)
            kw = kf * (scale * w_end)
            c_out[g, h] = decay * c0_ref[g, h] + lax.dot_general(
                kw.astype(BF16), v, TN, preferred_element_type=F32)
            n_out[g, h:h + 1, :] = decay * n0_ref[g, h:h + 1, :] + jnp.sum(kw, axis=0, keepdims=True)
            m_out[g, h:h + 1, :] = jnp.broadcast_to(m_end, (1, m_out.shape[2]))


def mlstm_sample(proj, gates_blk, bias, g_head, m0_blk, n0, c0, hc_buf, row_start,
                 heads, dk, dv, ts, group):
    hk, hv = heads * dk, heads * dv
    rows = group * ts
    nb = c0.shape[0]
    steps = nb // group
    assert row_start % rows == 0
    first = row_start // rows
    return pl.pallas_call(
        functools.partial(_mlstm_sample_kernel, heads=heads, dk=dk, dv=dv, ts=ts),
        grid=(steps,),
        in_specs=[pl.BlockSpec((rows, hk), lambda i: (i + first, 0)),
                  pl.BlockSpec((rows, hk), lambda i: (i + first, 1)),
                  pl.BlockSpec((rows, hv), lambda i: (i + first, 2 * hk // hv)),
                  pl.BlockSpec((rows, hv), lambda i: (i + first, 2 * hk // hv + 1)),
                  pl.BlockSpec((1, 2 * heads, rows), lambda i: (i, 0, 0)),
                  pl.BlockSpec((2 * heads, 1), lambda i: (0, 0)),
                  pl.BlockSpec((1, hv), lambda i: (0, 0)),
                  pl.BlockSpec((1, rows, heads), lambda i: (i, 0, 0)),
                  pl.BlockSpec((group, heads, dk), lambda i: (i, 0, 0)),
                  pl.BlockSpec((group, heads, dk, dv), lambda i: (i, 0, 0, 0)),
                  pl.BlockSpec(memory_space=pl.ANY)],
        out_specs=[pl.BlockSpec((rows, hv), lambda i: (i + first, 0)),
                   pl.BlockSpec((group, heads, dk, dv), lambda i: (i, 0, 0, 0)),
                   pl.BlockSpec((group, heads, dk), lambda i: (i, 0, 0)),
                   pl.BlockSpec((group, heads, LANES_V7X), lambda i: (i, 0, 0))],
        out_shape=[jax.ShapeDtypeStruct(hc_buf.shape, hc_buf.dtype),
                   jax.ShapeDtypeStruct((nb, heads, dk, dv), F32),
                   jax.ShapeDtypeStruct((nb, heads, dk), F32),
                   jax.ShapeDtypeStruct((nb, heads, LANES_V7X), F32)],
        input_output_aliases={10: 0},
        compiler_params=_params("arbitrary"),
        name="mlstm_sample",
    )(proj, proj, proj, proj, gates_blk, bias, g_head, m0_blk, n0, c0, hc_buf)


def _online_softmax(s, m_prev, c_exp):
    m_new = jnp.maximum(m_prev, jnp.max(s, axis=1, keepdims=True))
    alpha = jnp.exp2((m_prev - m_new) * c_exp)
    p = jnp.exp2((s - m_new) * c_exp)
    return m_new, alpha, p


def _attn_prompt_kernel(q_ref, kn_ref, kr_ref, v_ref, o_prev_ref, o_ref, kf_sc, vx_sc, m_sc,
                        acc_sc, *, c_exp, nope, vdim, parts):
    del o_prev_ref
    qi = pl.program_id(2)
    tq = q_ref.shape[0]
    tp = tq // parts

    @pl.when(qi == 0)
    def _():
        kf_sc[:, :nope] = kn_ref[...]
        kf_sc[:, nope:] = kr_ref[...]
        vx_sc[:, :vdim] = v_ref[...]
        lane = lax.broadcasted_iota(jnp.int32, (vx_sc.shape[0], vx_sc.shape[1] - vdim), 1)
        vx_sc[:, vdim:] = (lane == 0).astype(BF16)

    m_sc[...] = jnp.full_like(m_sc, -jnp.inf)
    acc_sc[...] = jnp.zeros_like(acc_sc)

    def update(part, k, vx, row_offset):
        s = lax.dot_general(q_ref[part * tp:(part + 1) * tp, :], k, NT, preferred_element_type=F32)
        if row_offset is not None:
            row = lax.broadcasted_iota(jnp.int32, s.shape, 0) + row_offset
            col = lax.broadcasted_iota(jnp.int32, s.shape, 1)
            s = jnp.where(col <= row, s, -jnp.inf)
        m_new, alpha, p = _online_softmax(s, m_sc[part], c_exp)
        acc_sc[part] = alpha * acc_sc[part] + jnp.dot(p.astype(BF16), vx,
                                                      preferred_element_type=F32)
        m_sc[part] = m_new

    def body(ki, carry):
        start = pl.multiple_of(ki * tq, tq)
        k = kf_sc[pl.ds(start, tq), :]
        vx = vx_sc[pl.ds(start, tq), :]
        for part in range(parts):
            update(part, k, vx, None)
        return carry

    lax.fori_loop(0, qi, body, 0)
    start = pl.multiple_of(qi * tq, tq)
    for part in range(parts):
        n_keys = (part + 1) * tp
        update(part, kf_sc[pl.ds(start, n_keys), :], vx_sc[pl.ds(start, n_keys), :], part * tp)
    for part in range(parts):
        acc = acc_sc[part]
        o_ref[part * tp:(part + 1) * tp, :] = (acc[:, :vdim] / acc[:, vdim:vdim + 1]).astype(o_ref.dtype)


def attn_prompt(q, kn, kr, v, o_buf, batch, seq, heads, nope, vdim, scale, tq, parts):
    hw = q.shape[1] // heads
    nq = seq // tq
    return pl.pallas_call(
        functools.partial(_attn_prompt_kernel, c_exp=scale * LOG2E, nope=nope, vdim=vdim,
                          parts=parts),
        grid=(batch, heads, nq),
        in_specs=[pl.BlockSpec((tq, hw), lambda b, h, i: (b * nq + i, h)),
                  pl.BlockSpec((seq, nope), lambda b, h, i: (b, h)),
                  pl.BlockSpec((seq, hw - nope), lambda b, h, i: (b, 0)),
                  pl.BlockSpec((seq, vdim), lambda b, h, i: (b, h)),
                  pl.BlockSpec(memory_space=pl.ANY)],
        out_specs=pl.BlockSpec((tq, vdim), lambda b, h, i: (b * nq + i, h)),
        out_shape=jax.ShapeDtypeStruct(o_buf.shape, o_buf.dtype),
        input_output_aliases={4: 0},
        scratch_shapes=[pltpu.VMEM((seq, hw), BF16),
                        pltpu.VMEM((seq, 2 * vdim), BF16),
                        pltpu.VMEM((parts, tq // parts, 1), F32),
                        pltpu.VMEM((parts, tq // parts, 2 * vdim), F32)],
        compiler_params=_params("arbitrary", "arbitrary", "arbitrary"),
        name="attn_prompt",
    )(q, kn, kr, v, o_buf)


def _qlat_kernel(q_ref, w_ref, o_ref, *, nope):
    q = q_ref[...]
    lat = lax.dot_general(q[:, :nope], w_ref[...].astype(BF16), NT, preferred_element_type=F32)
    o_ref[0] = jnp.concatenate([lat.astype(BF16), q[:, nope:]], axis=1)


def q_latent(q, w_uk2, heads, nope, row_start, n_rows):
    m = n_rows
    assert row_start % m == 0
    first = row_start // m
    hw = q.shape[1] // heads
    rank = w_uk2.shape[0]
    return pl.pallas_call(
        functools.partial(_qlat_kernel, nope=nope),
        grid=(heads,),
        in_specs=[pl.BlockSpec((m, hw), lambda h: (first, h)),
                  pl.BlockSpec((rank, nope), lambda h: (0, h))],
        out_specs=pl.BlockSpec((1, m, rank + hw - nope), lambda h: (h, 0, 0)),
        out_shape=jax.ShapeDtypeStruct((heads, m, rank + hw - nope), BF16),
        compiler_params=_params("arbitrary"),
        name="q_latent",
    )(q, w_uk2)


def _decode_kernel(pt_ref, q_ref, new_ref, ck_hbm, kr_hbm, o_ref, ck_buf, kr_buf, sems,
                   kc_sc, krt_sc, m_sc, l_sc, acc_sc, *, pages, c_exp, rank, rope, ts):
    b, j = pl.program_id(0), pl.program_id(1)
    nb, nj = pl.num_programs(0), pl.num_programs(1)
    psz = ck_buf.shape[2]
    step = b * nj + j
    slot = lax.rem(step, 2)

    def page_copies(bb, jj, sl):
        copies = []
        for p in range(pages):
            page = pt_ref[bb, jj * pages + p]
            copies.append(pltpu.make_async_copy(ck_hbm.at[page], ck_buf.at[sl, p], sems.at[0, sl]))
            copies.append(pltpu.make_async_copy(kr_hbm.at[page], kr_buf.at[sl, p], sems.at[1, sl]))
        return copies

    def start_all(copies):
        for c in copies:
            c.start()

    @pl.when(step == 0)
    def _():
        start_all(page_copies(b, j, slot))

    @pl.when(step + 1 < nb * nj)
    def _():
        last = j == nj - 1
        start_all(page_copies(jnp.where(last, b + 1, b), jnp.where(last, 0, j + 1), 1 - slot))

    for c in page_copies(b, j, slot):
        c.wait()

    @pl.when(j == 0)
    def _():
        m_sc[...] = jnp.full_like(m_sc, -jnp.inf)
        l_sc[...] = jnp.zeros_like(l_sc)
        acc_sc[...] = jnp.zeros_like(acc_sc)

    for p in range(pages):
        kc_sc[p * psz:(p + 1) * psz, :] = ck_buf[slot, p].astype(BF16)
        krt_sc[:, p * psz:(p + 1) * psz] = kr_buf[slot, p].astype(BF16)

    q = q_ref[0]
    s = (lax.dot_general(q[:, :rank], kc_sc[...], NT, preferred_element_type=F32)
         + jnp.dot(q[:, rank:rank + rope], krt_sc[...], preferred_element_type=F32))
    m, alpha, p = _online_softmax(s, m_sc[...], c_exp)
    l = alpha * l_sc[...] + jnp.sum(p, axis=1, keepdims=True)
    acc = alpha * acc_sc[...] + jnp.dot(p.astype(BF16), kc_sc[...], preferred_element_type=F32)
    m_sc[...] = m
    l_sc[...] = l
    acc_sc[...] = acc

    @pl.when(j == pl.num_programs(1) - 1)
    def _():
        new = new_ref[0]
        s2 = lax.dot_general(q, new, NT, preferred_element_type=F32)
        qrow = lax.broadcasted_iota(jnp.int32, s2.shape, 0)
        heads = q.shape[0] // ts
        tok = sum((qrow >= t * heads).astype(jnp.int32) for t in range(1, ts))
        key = lax.broadcasted_iota(jnp.int32, s2.shape, 1)
        s2 = jnp.where(key <= tok, s2, -jnp.inf)
        _, alpha, p = _online_softmax(s2, m, c_exp)
        l2 = alpha * l + jnp.sum(p, axis=1, keepdims=True)
        acc2 = alpha * acc + jnp.dot(p.astype(BF16), new[:, :rank], preferred_element_type=F32)
        o_ref[0] = acc2 / l2


def attn_decode(q_full, new_keys, cache_ckv, cache_krope_t, page_table, scale, ts, pages):
    nb, rows, width = q_full.shape
    _, psz, rank = cache_ckv.shape
    rope = cache_krope_t.shape[1]
    n_pages = page_table.shape[1]
    nj = n_pages // pages

    grid_spec = pltpu.PrefetchScalarGridSpec(
        num_scalar_prefetch=1,
        grid=(nb, nj),
        in_specs=[pl.BlockSpec((1, rows, width), lambda b, j, pt: (b, 0, 0)),
                  pl.BlockSpec((1, new_keys.shape[1], width), lambda b, j, pt: (b, 0, 0)),
                  pl.BlockSpec(memory_space=pl.ANY),
                  pl.BlockSpec(memory_space=pl.ANY)],
        out_specs=pl.BlockSpec((1, rows, rank), lambda b, j, pt: (b, 0, 0)),
        scratch_shapes=[pltpu.VMEM((2, pages, psz, rank), cache_ckv.dtype),
                        pltpu.VMEM((2, pages, rope, psz), cache_krope_t.dtype),
                        pltpu.SemaphoreType.DMA((2, 2)),
                        pltpu.VMEM((pages * psz, rank), BF16),
                        pltpu.VMEM((rope, pages * psz), BF16),
                        pltpu.VMEM((rows, 1), F32),
                        pltpu.VMEM((rows, 1), F32),
                        pltpu.VMEM((rows, rank), F32)],
    )
    return pl.pallas_call(
        functools.partial(_decode_kernel, pages=pages, c_exp=scale * LOG2E, rank=rank,
                          rope=rope, ts=ts),
        grid_spec=grid_spec,
        out_shape=jax.ShapeDtypeStruct((nb, rows, rank), F32),
        compiler_params=_params("arbitrary", "arbitrary"),
        name="attn_decode",
    )(page_table, q_full, new_keys, cache_ckv, cache_krope_t)


def _oup_kernel(o_ref, w_ref, prev_ref, out_ref):
    del prev_ref
    out_ref[...] = jnp.dot(o_ref[0], w_ref[...].astype(BF16),
                           preferred_element_type=F32).astype(out_ref.dtype)


def o_up(o_lat, w_uv2, vdim, o_buf, row_start):
    heads, m, rank = o_lat.shape
    assert row_start % m == 0
    first = row_start // m
    return pl.pallas_call(
        _oup_kernel,
        grid=(heads,),
        in_specs=[pl.BlockSpec((1, m, rank), lambda h: (h, 0, 0)),
                  pl.BlockSpec((rank, vdim), lambda h: (0, h)),
                  pl.BlockSpec(memory_space=pl.ANY)],
        out_specs=pl.BlockSpec((m, vdim), lambda h: (first, h)),
        out_shape=jax.ShapeDtypeStruct(o_buf.shape, o_buf.dtype),
        input_output_aliases={2: 0},
        compiler_params=_params("arbitrary"),
        name="o_up",
    )(o_lat, w_uv2, o_buf)


def _rope_tables(pos, rope_dim, nope_pad, width):
    half = rope_dim // 2
    freqs = ROPE_THETA ** (-jnp.arange(half, dtype=F32) / half)
    ang = pos.astype(F32)[:, None] * freqs[None, :]
    cos, sin = jnp.cos(ang), jnp.sin(ang)
    n = pos.shape[0]
    zeros = lambda w: jnp.zeros((n, w), F32)
    tail = width - nope_pad - rope_dim
    ta = jnp.concatenate([jnp.ones((n, nope_pad), F32), cos, cos, zeros(tail)], axis=1)
    tb = jnp.concatenate([zeros(nope_pad + half), sin, zeros(tail)], axis=1)
    tc = jnp.concatenate([zeros(nope_pad), -sin, zeros(half + tail)], axis=1)
    return ta, tb, tc


def kernel(x_prompt, x_sample, cache_ckv, cache_krope, page_table, state_C, state_n, state_m,
           g_mix, g_ffn, w_ffn_in, w_ffn_out, w_a_in, b_a_gates, g_a_head, w_a_out,
           g_kv_in, w_kv_down, g_kv_latent, w_uk, w_uv, w_q_down, g_q_latent, w_q_up,
           w_b_out, g_final):
    bp, tp, d = x_prompt.shape
    bs, ts, _ = x_sample.shape
    mp, ms = bp * tp, bs * ts
    n_a = state_C.shape[0]
    n_layers = g_mix.shape[0]
    assert n_a == 1 and n_layers == 2, "one mLSTM layer followed by one MLA layer"
    _, _, heads_a, dk, dv = state_C.shape
    hk, hv = heads_a * dk, heads_a * dv
    rank, heads_b, nope = w_uk.shape
    vdim = w_uv.shape[2]
    rope = cache_krope.shape[2]
    psz = cache_ckv.shape[1]
    past = page_table.shape[1] * psz
    hw = nope + 2 * rope
    assert nope % LANES_V7X == 0 and hw % LANES_V7X == 0 and (2 * rope) % LANES_V7X == 0
    attn_scale = (nope + rope) ** -0.5

    x = jnp.concatenate([x_prompt.reshape(mp, d), x_sample.reshape(ms, d)], axis=0)
    pos = jnp.concatenate([jnp.tile(jnp.arange(tp), bp), jnp.tile(past + jnp.arange(ts), bs)])
    tabs = _rope_tables(pos, rope, nope, hw)

    (xn,) = rmsnorm_rows(x, g_mix[0:1])
    w_in_t = jnp.swapaxes(w_a_in[0], 0, 1)
    proj = matmul(xn, w_in_t, n_out=2 * hk + 2 * hv, out_dtype=BF16, w_t=True,
                  name="mlstm_in_proj")
    gates = gates_t(xn, w_in_t[2 * hk + 2 * hv:])
    bias = b_a_gates[0].reshape(2 * heads_a, 1)
    g_head = g_a_head[0].reshape(1, hv)

    hc = jnp.zeros((mp + ms, hv), BF16)
    hc, c_p, n_p, m_p = mlstm_prompt(proj, gates, bias, g_head, hc, bp, tp, heads_a, dk, dv,
                                     chunk=min(512, tp))
    group = 8
    rows = group * ts
    gates_blk = gates[:, mp:].reshape(2 * heads_a, bs // group, rows).transpose(1, 0, 2)
    m0_blk = jnp.repeat(state_m[0], ts, axis=0).reshape(bs // group, rows, heads_a)
    hc, c_s, n_s, m_s = mlstm_sample(proj, gates_blk, bias, g_head, m0_blk, state_n[0],
                                     state_C[0], hc, mp, heads_a, dk, dv, ts, group)
    x = matmul(hc, w_a_out[0], res=x, name="mlstm_out_proj")

    (xn,) = rmsnorm_rows(x, g_ffn[0:1])
    x = matmul(swiglu_in(xn, w_ffn_in, 0), w_ffn_out, layer=0, res=x, w_buffers=1, name="ffn_out")

    xn_kv, xn_q = rmsnorm_rows(x, jnp.stack([g_kv_in, g_mix[1]]))
    w_kv_pad = jnp.pad(w_kv_down, ((0, 0), (0, rope)))
    g_lat = g_kv_latent.reshape(1, rank)
    ckv_p, ckvb_p, kr_p, krb_p = kv_down(xn_kv, w_kv_pad, g_lat, tabs, rank, rope, 0, mp)
    ckv_s, ckvb_s, kr_s, krb_s = kv_down(xn_kv, w_kv_pad, g_lat, tabs, rank, rope, mp, ms)

    xq = q_down(xn_q, w_q_down[0], g_q_latent[0].reshape(1, rank))
    w_qup_pad = jnp.pad(w_q_up[0].reshape(rank, heads_b, nope + rope),
                        ((0, 0), (0, 0), (0, rope))).reshape(rank, heads_b * hw)
    q = q_up(xq, w_qup_pad, tabs, hw)

    w_uk2 = w_uk.reshape(rank, heads_b * nope)
    w_uv2 = w_uv.reshape(rank, heads_b * vdim)
    kn = matmul(ckvb_p, w_uk2, out_dtype=BF16, name="k_up")
    v = matmul(ckvb_p, w_uv2, out_dtype=BF16, name="v_up")
    tq = min(2048, tp)
    o = jnp.zeros((mp + ms, heads_b * vdim), BF16)
    o = attn_prompt(q, kn, krb_p, v, o, bp, tp, heads_b, nope, vdim, attn_scale,
                    tq=tq, parts=max(1, tq // 256))

    q_lat = q_latent(q, w_uk2, heads_b, nope, mp, ms)
    width = rank + 2 * rope
    q_full = q_lat.reshape(heads_b, bs, ts, width).transpose(1, 2, 0, 3).reshape(bs, ts * heads_b, width)
    new_keys = jnp.concatenate([ckvb_s, krb_s], axis=1).reshape(bs, ts, width)
    new_keys = jnp.pad(new_keys, ((0, 0), (0, BF16_SUBLANES_V7X - ts), (0, 0)))
    cache_krope_t = jnp.swapaxes(cache_krope, 1, 2)
    o_lat = attn_decode(q_full, new_keys, cache_ckv, cache_krope_t, page_table, attn_scale, ts,
                        pages=min(64, page_table.shape[1]))
    o_lat = o_lat.reshape(bs, ts, heads_b, rank).transpose(2, 0, 1, 3).reshape(heads_b, ms, rank)
    o = o_up(o_lat.astype(BF16), w_uv2, vdim, o, mp)

    x = matmul(o, w_b_out[0], res=x, name="mla_out_proj")

    (xn,) = rmsnorm_rows(x, g_ffn[1:2])
    x = matmul(swiglu_in(xn, w_ffn_in, 1), w_ffn_out, layer=1, res=x, w_buffers=1, name="ffn_out")
    g_fin = g_final.reshape(1, d)
    (y_p,) = rmsnorm_rows(x, g_fin, out_dtype=F32, row_start=0, n_rows=mp)
    (y_s,) = rmsnorm_rows(x, g_fin, out_dtype=F32, row_start=mp, n_rows=ms)

    return (y_p.reshape(bp, tp, d), y_s.reshape(bs, ts, d),
            c_p[None], n_p[None], m_p[None, :, :, 0],
            ckv_p.reshape(bp, tp, rank), kr_p.reshape(bp, tp, rope),
            c_s[None], n_s[None], m_s[None, :, :, 0],
            ckv_s.reshape(bs, ts, rank), kr_s.reshape(bs, ts, rope))
```

```python
import functools

import jax
import jax.numpy as jnp
from jax import lax
from jax.experimental import pallas as pl
from jax.experimental.pallas import tpu as pltpu

F32 = jnp.float32
BF16 = jnp.bfloat16

EPS = 1e-6
GATE_CAP = 15.0
ROPE_THETA = 10000.0
LOG2E = 1.4426950408889634

LANES_V7X = 128
BF16_SUBLANES_V7X = 16
VMEM_BYTES_V7X = 64 * 1024 * 1024
VMEM_BUDGET = VMEM_BYTES_V7X * 5 // 8
VMEM_LIMIT = VMEM_BYTES_V7X * 7 // 8

NT = (((1,), (1,)), ((), ()))
TN = (((0,), (0,)), ((), ()))


def _params(*sem):
    return pltpu.CompilerParams(dimension_semantics=sem, vmem_limit_bytes=VMEM_LIMIT)


def _largest_divisor(n, cap, mult):
    best = None
    for d in range(mult, min(n, cap) + 1, mult):
        if n % d == 0:
            best = d
    return n if best is None else best


def _row_tile(m, cap=1088):
    return _largest_divisor(m, cap, BF16_SUBLANES_V7X)


def _rms_kernel(x_ref, g_ref, *o_refs):
    x = x_ref[...]
    y = x * lax.rsqrt(jnp.mean(x * x, axis=-1, keepdims=True) + EPS)
    for i, o_ref in enumerate(o_refs):
        o_ref[...] = (y * g_ref[i:i + 1, :]).astype(o_ref.dtype)


def rmsnorm_rows(x, gains, out_dtype=BF16, row_start=0, n_rows=None):
    d = x.shape[1]
    m = x.shape[0] if n_rows is None else n_rows
    n = gains.shape[0]
    tm = _largest_divisor(m, 512, BF16_SUBLANES_V7X)
    assert row_start % tm == 0
    first = row_start // tm
    outs = pl.pallas_call(
        _rms_kernel,
        grid=(m // tm,),
        in_specs=[pl.BlockSpec((tm, d), lambda i: (i + first, 0)),
                  pl.BlockSpec((n, d), lambda i: (0, 0))],
        out_specs=[pl.BlockSpec((tm, d), lambda i: (i, 0))] * n,
        out_shape=[jax.ShapeDtypeStruct((m, d), out_dtype)] * n,
        compiler_params=_params("arbitrary"),
        name="rmsnorm_rows",
    )(x, gains)
    return outs


def _mm_tiles(m, k, n, out_bytes, n_w=1, has_res=False, w_buffers=2, tn_max=512):
    best = None
    for tn in (1024, 512, 256, 128):
        if n % tn or tn > tn_max:
            continue
        for tm in range(BF16_SUBLANES_V7X, min(m, 1088) + 1, BF16_SUBLANES_V7X):
            if m % tm:
                continue
            need = (2 * tm * k * 2 + n_w * (w_buffers * k * tn * 4 + k * tn * 2)
                    + 2 * tm * tn * out_bytes + (2 * tm * tn * 4 if has_res else 0))
            if need <= VMEM_BUDGET and (best is None or tm * tn > best[0] * best[1]):
                best = (tm, tn)
    assert best is not None, (m, k, n)
    return best


def _mm_kernel(a_ref, w_ref, *rest, has_res, w_t):
    if has_res:
        r_ref, o_ref, w_sc = rest
    else:
        o_ref, w_sc = rest

    @pl.when(pl.program_id(1) == 0)
    def _():
        w_sc[...] = w_ref[...].astype(BF16)

    if w_t:
        acc = lax.dot_general(a_ref[...], w_sc[...], NT, preferred_element_type=F32)
    else:
        acc = jnp.dot(a_ref[...], w_sc[...], preferred_element_type=F32)
    if has_res:
        acc = acc + r_ref[...]
    o_ref[...] = acc.astype(o_ref.dtype)


def _layer_spec(block, index_map, layer, buffers=2):
    mode = {} if buffers == 2 else {"pipeline_mode": pl.Buffered(buffers)}
    if layer is None:
        return pl.BlockSpec(block, index_map, **mode)
    return pl.BlockSpec((None,) + block, lambda *g: (layer,) + index_map(*g), **mode)


def matmul(a, w, n_out=None, res=None, out_dtype=F32, w_t=False, layer=None, w_buffers=2,
           tn_max=512, name="matmul"):
    m, k = a.shape
    w_shape = w.shape if layer is None else w.shape[1:]
    n = (w_shape[0] if w_t else w_shape[1]) if n_out is None else n_out
    tm, tn = _mm_tiles(m, k, n, jnp.dtype(out_dtype).itemsize, has_res=res is not None,
                       w_buffers=w_buffers, tn_max=tn_max)
    w_block = (tn, k) if w_t else (k, tn)
    w_map = (lambda j, i: (j, 0)) if w_t else (lambda j, i: (0, j))
    in_specs = [pl.BlockSpec((tm, k), lambda j, i: (i, 0)),
                _layer_spec(w_block, w_map, layer, w_buffers)]
    args = [a, w]
    if res is not None:
        in_specs.append(pl.BlockSpec((tm, tn), lambda j, i: (i, j)))
        args.append(res)
    return pl.pallas_call(
        functools.partial(_mm_kernel, has_res=res is not None, w_t=w_t),
        grid=(n // tn, m // tm),
        in_specs=in_specs,
        out_specs=pl.BlockSpec((tm, tn), lambda j, i: (i, j)),
        out_shape=jax.ShapeDtypeStruct((m, n), out_dtype),
        scratch_shapes=[pltpu.VMEM(w_block, BF16)],
        compiler_params=_params("arbitrary", "arbitrary"),
        name=name,
    )(*args)


def _proj_norm_kernel(a_ref, w_ref, r_ref, g_ref, x_ref, *rest):
    xn_refs, w_sc = rest[:-1], rest[-1]

    @pl.when(pl.program_id(0) == 0)
    def _():
        w_sc[...] = w_ref[...].astype(BF16)

    x = r_ref[...] + jnp.dot(a_ref[...], w_sc[...], preferred_element_type=F32)
    x_ref[...] = x
    y = x * lax.rsqrt(jnp.mean(x * x, axis=-1, keepdims=True) + EPS)
    for i, o_ref in enumerate(xn_refs):
        o_ref[...] = (y * g_ref[i:i + 1, :]).astype(o_ref.dtype)


def proj_res_norm(a, w, res, gains, name):
    m, k = a.shape
    n = w.shape[1]
    ng = gains.shape[0]
    tm = _row_tile(m, 272)
    row = lambda i: (i, 0)
    outs = pl.pallas_call(
        _proj_norm_kernel,
        grid=(m // tm,),
        in_specs=[pl.BlockSpec((tm, k), row),
                  pl.BlockSpec((k, n), lambda i: (0, 0), pipeline_mode=pl.Buffered(1)),
                  pl.BlockSpec((tm, n), row),
                  pl.BlockSpec((ng, n), lambda i: (0, 0))],
        out_specs=[pl.BlockSpec((tm, n), row)] * (1 + ng),
        out_shape=[jax.ShapeDtypeStruct((m, n), F32)] + [jax.ShapeDtypeStruct((m, n), BF16)] * ng,
        scratch_shapes=[pltpu.VMEM((k, n), BF16)],
        compiler_params=_params("arbitrary"),
        name=name,
    )(a, w, res, gains)
    return outs


def _swiglu_kernel(a_ref, wg_ref, wu_ref, o_ref, wg_sc, wu_sc):
    @pl.when(pl.program_id(1) == 0)
    def _():
        wg_sc[...] = wg_ref[...].astype(BF16)
        wu_sc[...] = wu_ref[...].astype(BF16)

    a = a_ref[...]
    g = jnp.dot(a, wg_sc[...], preferred_element_type=F32)
    u = jnp.dot(a, wu_sc[...], preferred_element_type=F32)
    o_ref[...] = (g * jax.nn.sigmoid(g) * u).astype(o_ref.dtype)


def swiglu_in(a, w_in, layer):
    m, k = a.shape
    f = w_in.shape[2] // 2
    tm, tf = _mm_tiles(m, k, f, 2, n_w=2)
    nf = f // tf
    return pl.pallas_call(
        _swiglu_kernel,
        grid=(nf, m // tm),
        in_specs=[pl.BlockSpec((tm, k), lambda j, i: (i, 0)),
                  _layer_spec((k, tf), lambda j, i: (0, j), layer),
                  _layer_spec((k, tf), lambda j, i: (0, j + nf), layer)],
        out_specs=pl.BlockSpec((tm, tf), lambda j, i: (i, j)),
        out_shape=jax.ShapeDtypeStruct((m, f), BF16),
        scratch_shapes=[pltpu.VMEM((k, tf), BF16), pltpu.VMEM((k, tf), BF16)],
        compiler_params=_params("arbitrary", "arbitrary"),
        name="swiglu_in",
    )(a, w_in, w_in)


def _gates_kernel(wt_ref, a_ref, o_ref):
    o_ref[...] = lax.dot_general(wt_ref[...].astype(BF16), a_ref[...], NT,
                                 preferred_element_type=F32)


def gates_t(a, w_gate_t):
    m, k = a.shape
    g = w_gate_t.shape[0]
    tm = _largest_divisor(m, 512, LANES_V7X)
    return pl.pallas_call(
        _gates_kernel,
        grid=(m // tm,),
        in_specs=[pl.BlockSpec((g, k), lambda i: (0, 0)),
                  pl.BlockSpec((tm, k), lambda i: (i, 0))],
        out_specs=pl.BlockSpec((g, tm), lambda i: (0, i)),
        out_shape=jax.ShapeDtypeStruct((g, m), F32),
        compiler_params=_params("arbitrary"),
        name="gates_t",
    )(w_gate_t, a)


def _rope_lanes(x, a, b, c):
    w = x.shape[1]
    return x * a + pltpu.roll(x, 32, 1) * b + pltpu.roll(x, w - 32, 1) * c


def _qdown_kernel(a_ref, w_ref, g_ref, o_ref, w_sc):
    @pl.when(pl.program_id(0) == 0)
    def _():
        w_sc[...] = w_ref[...].astype(BF16)

    y = jnp.dot(a_ref[...], w_sc[...], preferred_element_type=F32)
    y = y * lax.rsqrt(jnp.mean(y * y, axis=-1, keepdims=True) + EPS)
    o_ref[...] = (y * g_ref[...]).astype(o_ref.dtype)


def q_down(a, w, g):
    m, k = a.shape
    r = w.shape[1]
    tm = _row_tile(m)
    return pl.pallas_call(
        _qdown_kernel,
        grid=(m // tm,),
        in_specs=[pl.BlockSpec((tm, k), lambda i: (i, 0)),
                  pl.BlockSpec((k, r), lambda i: (0, 0)),
                  pl.BlockSpec((1, r), lambda i: (0, 0))],
        out_specs=pl.BlockSpec((tm, r), lambda i: (i, 0)),
        out_shape=jax.ShapeDtypeStruct((m, r), BF16),
        scratch_shapes=[pltpu.VMEM((k, r), BF16)],
        compiler_params=_params("arbitrary"),
        name="q_down",
    )(a, w, g)


def _qup_kernel(a_ref, w_ref, ta_ref, tb_ref, tc_ref, o_ref, w_sc, *, hw):
    @pl.when(pl.program_id(0) == 0)
    def _():
        w_sc[...] = w_ref[...].astype(BF16)

    a = a_ref[...]
    ta, tb, tc = ta_ref[...], tb_ref[...], tc_ref[...]
    rw = ta.shape[1]
    for h in range(w_sc.shape[1] // hw):
        y = jnp.dot(a, w_sc[:, h * hw:(h + 1) * hw], preferred_element_type=F32)
        o_ref[:, h * hw:(h + 1) * hw - rw] = y[:, :hw - rw].astype(o_ref.dtype)
        o_ref[:, (h + 1) * hw - rw:(h + 1) * hw] = _rope_lanes(y[:, hw - rw:], ta, tb, tc).astype(o_ref.dtype)


def q_up(a, w_pad, tabs, hw, rw):
    m, k = a.shape
    n = w_pad.shape[1]
    tm = _row_tile(m, 544)
    assert tabs[0].shape[1] == rw
    tspec = pl.BlockSpec((tm, rw), lambda i: (i, 0))
    return pl.pallas_call(
        functools.partial(_qup_kernel, hw=hw),
        grid=(m // tm,),
        in_specs=[pl.BlockSpec((tm, k), lambda i: (i, 0)),
                  pl.BlockSpec((k, n), lambda i: (0, 0)),
                  tspec, tspec, tspec],
        out_specs=pl.BlockSpec((tm, n), lambda i: (i, 0)),
        out_shape=jax.ShapeDtypeStruct((m, n), BF16),
        scratch_shapes=[pltpu.VMEM((k, n), BF16)],
        compiler_params=_params("arbitrary"),
        name="q_up",
    )(a, w_pad, *tabs)


def _kvdown_kernel(a_ref, w_ref, g_ref, ta_ref, tb_ref, tc_ref,
                   ckv_ref, ckvb_ref, kr_ref, krb_ref, w_sc, *, rank):
    @pl.when(pl.program_id(0) == 0)
    def _():
        w_sc[...] = w_ref[...].astype(BF16)

    y = jnp.dot(a_ref[...], w_sc[...], preferred_element_type=F32)
    lat = y[:, :rank]
    lat = lat * lax.rsqrt(jnp.mean(lat * lat, axis=-1, keepdims=True) + EPS) * g_ref[...]
    ckv_ref[...] = lat
    ckvb_ref[...] = lat.astype(BF16)
    kr = _rope_lanes(y[:, rank:], ta_ref[...], tb_ref[...], tc_ref[...])
    kr_ref[...] = kr[:, :kr_ref.shape[1]]
    krb_ref[...] = kr.astype(BF16)


def kv_down(a, w_pad, g, tabs, rank, rope, row_start, n_rows):
    k = a.shape[1]
    m = n_rows
    n = w_pad.shape[1]
    rw = n - rank
    tm = _row_tile(m)
    assert row_start % tm == 0
    first = row_start // tm
    assert tabs[0].shape[1] == rw
    tspec = pl.BlockSpec((tm, rw), lambda i: (i + first, 0))
    return pl.pallas_call(
        functools.partial(_kvdown_kernel, rank=rank),
        grid=(m // tm,),
        in_specs=[pl.BlockSpec((tm, k), lambda i: (i + first, 0)),
                  pl.BlockSpec((k, n), lambda i: (0, 0)),
                  pl.BlockSpec((1, rank), lambda i: (0, 0)),
                  tspec, tspec, tspec],
        out_specs=[pl.BlockSpec((tm, rank), lambda i: (i, 0)),
                   pl.BlockSpec((tm, rank), lambda i: (i, 0)),
                   pl.BlockSpec((tm, rope), lambda i: (i, 0)),
                   pl.BlockSpec((tm, rw), lambda i: (i, 0))],
        out_shape=[jax.ShapeDtypeStruct((m, rank), F32),
                   jax.ShapeDtypeStruct((m, rank), BF16),
                   jax.ShapeDtypeStruct((m, rope), F32),
                   jax.ShapeDtypeStruct((m, rw), BF16)],
        scratch_shapes=[pltpu.VMEM((k, n), BF16)],
        compiler_params=_params("arbitrary"),
        name="kv_down",
    )(a, w_pad, g, *tabs)


def _split3(x):
    hi = x.astype(BF16)
    r = x - hi.astype(F32)
    mid = r.astype(BF16)
    lo = (r - mid.astype(F32)).astype(BF16)
    return hi, mid, lo


def _dot_exact(x, m01):
    return sum(jnp.dot(p, m01, preferred_element_type=F32) for p in _split3(x))


def _transpose_exact(eye, x):
    return sum(lax.dot_general(eye, p, NT, preferred_element_type=F32) for p in _split3(x))


def _log_sigmoid(x):
    return jnp.minimum(x, 0.0) - jnp.log1p(jnp.exp(-jnp.abs(x)))


def _softcap(z):
    return GATE_CAP * jnp.tanh(z / GATE_CAP)


def _mlstm_prompt_kernel(q_ref, k_ref, v_ref, o_ref, gt_ref, bias_ref, gh_ref, hc_prev_ref,
                         hc_ref, c_out, n_out, m_out, c_sc, n_sc, m_sc, *, heads, dk, dv):
    del hc_prev_ref
    c = pl.program_id(1)
    L = q_ref.shape[0]
    scale = dk ** -0.5

    @pl.when(c == 0)
    def _():
        c_sc[...] = jnp.zeros_like(c_sc)
        n_sc[...] = jnp.zeros_like(n_sc)
        m_sc[...] = jnp.zeros_like(m_sc)

    z = _softcap(gt_ref[...] + bias_ref[...])
    li_r = z[:heads]
    lf_r = _log_sigmoid(z[heads:])
    row = lax.broadcasted_iota(jnp.int32, (L, L), 0)
    col = lax.broadcasted_iota(jnp.int32, (L, L), 1)
    causal = col <= row
    upper = (row <= col).astype(BF16)
    eye = (row == col).astype(BF16)
    b_r = _dot_exact(lf_r, upper)
    cols = _transpose_exact(eye, jnp.concatenate([li_r, b_r], axis=0))

    for h in range(heads):
        q = q_ref[:, h * dk:(h + 1) * dk]
        k = k_ref[:, h * dk:(h + 1) * dk]
        v = v_ref[:, h * dv:(h + 1) * dv]
        li_c = cols[:, h:h + 1]
        b_c = cols[:, heads + h:heads + h + 1]
        m_prev = m_sc[h:h + 1, 0:1]
        n_prev = n_sc[h:h + 1, :]
        c_prev = c_sc[h]

        logw = jnp.where(causal, b_c - b_r[h:h + 1, :] + li_r[h:h + 1, :], -jnp.inf)
        log_inter = b_c + m_prev
        m_t = jnp.maximum(log_inter, jnp.max(logw, axis=1, keepdims=True))
        s = lax.dot_general(q, k, NT, preferred_element_type=F32) * scale * jnp.exp(logw - m_t)
        w_inter = jnp.exp(log_inter - m_t)
        num = (jnp.dot(s.astype(BF16), v, preferred_element_type=F32)
               + w_inter * jnp.dot(q, c_prev.astype(BF16), preferred_element_type=F32))
        qn = jnp.sum(q.astype(F32) * n_prev, axis=1, keepdims=True)
        den = jnp.sum(s, axis=1, keepdims=True) + w_inter * qn
        hh = num / jnp.maximum(jnp.abs(den), jnp.exp(-m_t))
        hh = hh * lax.rsqrt(jnp.mean(hh * hh, axis=1, keepdims=True) + EPS)
        og = o_ref[:, h * dv:(h + 1) * dv].astype(F32)
        hc_ref[:, h * dv:(h + 1) * dv] = (
            hh * gh_ref[:, h * dv:(h + 1) * dv] * jax.nn.sigmoid(og)).astype(hc_ref.dtype)

        m_end = m_t[L - 1:L, :]
        b_last = b_c[L - 1:L, :]
        w_end = jnp.exp(b_last - b_c + li_c - m_end)
        decay = jnp.exp(b_last + m_prev - m_end)
        kw = k.astype(F32) * (scale * w_end)
        c_sc[h] = decay * c_prev + lax.dot_general(kw.astype(BF16), v, TN,
                                                   preferred_element_type=F32)
        n_sc[h:h + 1, :] = decay * n_prev + jnp.sum(kw, axis=0, keepdims=True)
        m_sc[h:h + 1, :] = jnp.broadcast_to(m_end, (1, m_sc.shape[1]))

    @pl.when(c == pl.num_programs(1) - 1)
    def _():
        c_out[0] = c_sc[...]
        n_out[0] = n_sc[...]
        m_out[0] = m_sc[...]


def mlstm_prompt(proj, gates, bias, g_head, hc_buf, batch, seq, heads, dk, dv, chunk):
    hk, hv = heads * dk, heads * dv
    nc = seq // chunk
    row = lambda b, c: b * nc + c
    return pl.pallas_call(
        functools.partial(_mlstm_prompt_kernel, heads=heads, dk=dk, dv=dv),
        grid=(batch, nc),
        in_specs=[pl.BlockSpec((chunk, hk), lambda b, c: (row(b, c), 0)),
                  pl.BlockSpec((chunk, hk), lambda b, c: (row(b, c), 1)),
                  pl.BlockSpec((chunk, hv), lambda b, c: (row(b, c), 2 * hk // hv)),
                  pl.BlockSpec((chunk, hv), lambda b, c: (row(b, c), 2 * hk // hv + 1)),
                  pl.BlockSpec((2 * heads, chunk), lambda b, c: (0, row(b, c))),
                  pl.BlockSpec((2 * heads, 1), lambda b, c: (0, 0)),
                  pl.BlockSpec((1, hv), lambda b, c: (0, 0)),
                  pl.BlockSpec(memory_space=pl.ANY)],
        out_specs=[pl.BlockSpec((chunk, hv), lambda b, c: (row(b, c), 0)),
                   pl.BlockSpec((1, heads, dk, dv), lambda b, c: (b, 0, 0, 0)),
                   pl.BlockSpec((1, heads, dk), lambda b, c: (b, 0, 0)),
                   pl.BlockSpec((1, heads, LANES_V7X), lambda b, c: (b, 0, 0))],
        out_shape=[jax.ShapeDtypeStruct(hc_buf.shape, hc_buf.dtype),
                   jax.ShapeDtypeStruct((batch, heads, dk, dv), F32),
                   jax.ShapeDtypeStruct((batch, heads, dk), F32),
                   jax.ShapeDtypeStruct((batch, heads, LANES_V7X), F32)],
        scratch_shapes=[pltpu.VMEM((heads, dk, dv), F32),
                        pltpu.VMEM((heads, dk), F32),
                        pltpu.VMEM((heads, LANES_V7X), F32)],
        input_output_aliases={7: 0},
        compiler_params=_params("arbitrary", "arbitrary"),
        name="mlstm_prompt",
    )(proj, proj, proj, proj, gates, bias, g_head, hc_buf)


def _mlstm_sample_kernel(q_ref, k_ref, v_ref, o_ref, gt_ref, bias_ref, gh_ref, m0_ref,
                         n0_ref, c0_ref, hc_prev_ref, hc_ref, c_out, n_out, m_out,
                         *, heads, dk, dv, ts):
    del hc_prev_ref
    R = q_ref.shape[0]
    G = R // ts
    scale = dk ** -0.5

    z = _softcap(gt_ref[0] + bias_ref[...])
    li_r = z[:heads]
    lf_r = _log_sigmoid(z[heads:])
    row = lax.broadcasted_iota(jnp.int32, (R, R), 0)
    col = lax.broadcasted_iota(jnp.int32, (R, R), 1)
    seq_id = lambda i: sum((i >= g * ts).astype(jnp.int32) for g in range(1, G))
    same = seq_id(row) == seq_id(col)
    causal = same & (col <= row)
    upper = (same & (row <= col)).astype(BF16)
    eye = (row == col).astype(BF16)
    b_r = _dot_exact(lf_r, upper)
    cols = _transpose_exact(eye, jnp.concatenate([li_r, b_r], axis=0))
    seq_of_row = seq_id(lax.broadcasted_iota(jnp.int32, (R, 1), 0))
    m0 = m0_ref[0]

    for h in range(heads):
        q = q_ref[:, h * dk:(h + 1) * dk]
        k = k_ref[:, h * dk:(h + 1) * dk]
        v = v_ref[:, h * dv:(h + 1) * dv]
        li_c = cols[:, h:h + 1]
        b_c = cols[:, heads + h:heads + h + 1]
        m_prev = m0[:, h:h + 1]

        logw = jnp.where(causal, b_c - b_r[h:h + 1, :] + li_r[h:h + 1, :], -jnp.inf)
        log_inter = b_c + m_prev
        m_t = jnp.maximum(log_inter, jnp.max(logw, axis=1, keepdims=True))
        s = lax.dot_general(q, k, NT, preferred_element_type=F32) * scale * jnp.exp(logw - m_t)
        w_inter = jnp.exp(log_inter - m_t)

        qc = jnp.zeros((R, dv), F32)
        qn = jnp.zeros((R, 1), F32)
        qf = q.astype(F32)
        for g in range(G):
            mine = seq_of_row == g
            qc = jnp.where(mine, jnp.dot(q, c0_ref[g, h].astype(BF16),
                                         preferred_element_type=F32), qc)
            qn = jnp.where(mine, jnp.sum(qf * n0_ref[g, h:h + 1, :], axis=1, keepdims=True), qn)
        num = jnp.dot(s.astype(BF16), v, preferred_element_type=F32) + w_inter * qc
        den = jnp.sum(s, axis=1, keepdims=True) + w_inter * qn
        hh = num / jnp.maximum(jnp.abs(den), jnp.exp(-m_t))
        hh = hh * lax.rsqrt(jnp.mean(hh * hh, axis=1, keepdims=True) + EPS)
        og = o_ref[:, h * dv:(h + 1) * dv].astype(F32)
        hc_ref[:, h * dv:(h + 1) * dv] = (
            hh * gh_ref[:, h * dv:(h + 1) * dv] * jax.nn.sigmoid(og)).astype(hc_ref.dtype)

        kf = k.astype(F32)
        for g in range(G):
            last = (g + 1) * ts - 1
            m_end = m_t[last:last + 1, :]
            b_last = b_c[last:last + 1, :]
            mine = seq_of_row == g
            w_end = jnp.where(mine, jnp.exp(b_last - b_c + li_c - m_end), 0.0)
            decay = jnp.exp(b_last + m_prev[last:last + 1, :] - m_end)
            kw = kf * (scale * w_end)
            c_out[g, h] = decay * c0_ref[g, h] + lax.dot_general(
                kw.astype(BF16), v, TN, preferred_element_type=F32)
            n_out[g, h:h + 1, :] = decay * n0_ref[g, h:h + 1, :] + jnp.sum(kw, axis=0, keepdims=True)
            m_out[g, h:h + 1, :] = jnp.broadcast_to(m_end, (1, m_out.shape[2]))


def mlstm_sample(proj, gates_blk, bias, g_head, m0_blk, n0, c0, hc_buf, row_start,
                 heads, dk, dv, ts, group):
    hk, hv = heads * dk, heads * dv
    rows = group * ts
    nb = c0.shape[0]
    steps = nb // group
    assert row_start % rows == 0
    first = row_start // rows
    return pl.pallas_call(
        functools.partial(_mlstm_sample_kernel, heads=heads, dk=dk, dv=dv, ts=ts),
        grid=(steps,),
        in_specs=[pl.BlockSpec((rows, hk), lambda i: (i + first, 0)),
                  pl.BlockSpec((rows, hk), lambda i: (i + first, 1)),
                  pl.BlockSpec((rows, hv), lambda i: (i + first, 2 * hk // hv)),
                  pl.BlockSpec((rows, hv), lambda i: (i + first, 2 * hk // hv + 1)),
                  pl.BlockSpec((1, 2 * heads, rows), lambda i: (i, 0, 0)),
                  pl.BlockSpec((2 * heads, 1), lambda i: (0, 0)),
                  pl.BlockSpec((1, hv), lambda i: (0, 0)),
                  pl.BlockSpec((1, rows, heads), lambda i: (i, 0, 0)),
                  pl.BlockSpec((group, heads, dk), lambda i: (i, 0, 0)),
                  pl.BlockSpec((group, heads, dk, dv), lambda i: (i, 0, 0, 0)),
                  pl.BlockSpec(memory_space=pl.ANY)],
        out_specs=[pl.BlockSpec((rows, hv), lambda i: (i + first, 0)),
                   pl.BlockSpec((group, heads, dk, dv), lambda i: (i, 0, 0, 0)),
                   pl.BlockSpec((group, heads, dk), lambda i: (i, 0, 0)),
                   pl.BlockSpec((group, heads, LANES_V7X), lambda i: (i, 0, 0))],
        out_shape=[jax.ShapeDtypeStruct(hc_buf.shape, hc_buf.dtype),
                   jax.ShapeDtypeStruct((nb, heads, dk, dv), F32),
                   jax.ShapeDtypeStruct((nb, heads, dk), F32),
                   jax.ShapeDtypeStruct((nb, heads, LANES_V7X), F32)],
        input_output_aliases={10: 0},
        compiler_params=_params("arbitrary"),
        name="mlstm_sample",
    )(proj, proj, proj, proj, gates_blk, bias, g_head, m0_blk, n0, c0, hc_buf)


def _online_softmax(s, m_prev, c_exp):
    m_new = jnp.maximum(m_prev, jnp.max(s, axis=1, keepdims=True))
    alpha = jnp.exp2((m_prev - m_new) * c_exp)
    p = jnp.exp2((s - m_new) * c_exp)
    return m_new, alpha, p


def _attn_prompt_kernel(q_ref, kn_ref, kr_ref, v_ref, o_prev_ref, o_ref, kf_sc, vx_sc, m_sc,
                        acc_sc, *, c_exp, nope, vdim, parts):
    del o_prev_ref
    qi = pl.program_id(2)
    tq = q_ref.shape[0]
    tp = tq // parts

    @pl.when(qi == 0)
    def _():
        kf_sc[:, :nope] = kn_ref[...]
        kf_sc[:, nope:] = kr_ref[...]
        vx_sc[:, :vdim] = v_ref[...]
        lane = lax.broadcasted_iota(jnp.int32, (vx_sc.shape[0], vx_sc.shape[1] - vdim), 1)
        vx_sc[:, vdim:] = (lane == 0).astype(BF16)

    m_sc[...] = jnp.full_like(m_sc, -jnp.inf)
    acc_sc[...] = jnp.zeros_like(acc_sc)

    def update(part, k, vx, row_offset):
        s = lax.dot_general(q_ref[part * tp:(part + 1) * tp, :], k, NT, preferred_element_type=F32)
        if row_offset is not None:
            row = lax.broadcasted_iota(jnp.int32, s.shape, 0) + row_offset
            col = lax.broadcasted_iota(jnp.int32, s.shape, 1)
            s = jnp.where(col <= row, s, -jnp.inf)
        m_new, alpha, p = _online_softmax(s, m_sc[part], c_exp)
        acc_sc[part] = alpha * acc_sc[part] + jnp.dot(p.astype(BF16), vx,
                                                      preferred_element_type=F32)
        m_sc[part] = m_new

    def body(ki, carry):
        start = pl.multiple_of(ki * tq, tq)
        k = kf_sc[pl.ds(start, tq), :]
        vx = vx_sc[pl.ds(start, tq), :]
        for part in range(parts):
            update(part, k, vx, None)
        return carry

    lax.fori_loop(0, qi, body, 0)
    start = pl.multiple_of(qi * tq, tq)
    for part in range(parts):
        n_keys = (part + 1) * tp
        update(part, kf_sc[pl.ds(start, n_keys), :], vx_sc[pl.ds(start, n_keys), :], part * tp)
    for part in range(parts):
        acc = acc_sc[part]
        o_ref[part * tp:(part + 1) * tp, :] = (acc[:, :vdim] / acc[:, vdim:vdim + 1]).astype(o_ref.dtype)


def attn_prompt(q, kv, kr, o_buf, batch, seq, heads, nope, vdim, scale, tq, parts):
    hw = q.shape[1] // heads
    nq = seq // tq
    assert (heads * nope) % vdim == 0
    v_col0 = heads * nope // vdim
    return pl.pallas_call(
        functools.partial(_attn_prompt_kernel, c_exp=scale * LOG2E, nope=nope, vdim=vdim,
                          parts=parts),
        grid=(batch, heads, nq),
        in_specs=[pl.BlockSpec((tq, hw), lambda b, h, i: (b * nq + i, h)),
                  pl.BlockSpec((seq, nope), lambda b, h, i: (b, h)),
                  pl.BlockSpec((seq, hw - nope), lambda b, h, i: (b, 0)),
                  pl.BlockSpec((seq, vdim), lambda b, h, i: (b, v_col0 + h)),
                  pl.BlockSpec(memory_space=pl.ANY)],
        out_specs=pl.BlockSpec((tq, vdim), lambda b, h, i: (b * nq + i, h)),
        out_shape=jax.ShapeDtypeStruct(o_buf.shape, o_buf.dtype),
        input_output_aliases={4: 0},
        scratch_shapes=[pltpu.VMEM((seq, hw), BF16),
                        pltpu.VMEM((seq, 2 * vdim), BF16),
                        pltpu.VMEM((parts, tq // parts, 1), F32),
                        pltpu.VMEM((parts, tq // parts, 2 * vdim), F32)],
        compiler_params=_params("arbitrary", "arbitrary", "arbitrary"),
        name="attn_prompt",
    )(q, kv, kr, kv, o_buf)


def _qlat_kernel(q_ref, w_ref, o_ref, *, nope):
    q = q_ref[...]
    lat = lax.dot_general(q[:, :nope], w_ref[...].astype(BF16), NT, preferred_element_type=F32)
    o_ref[0] = jnp.concatenate([lat.astype(BF16), q[:, nope:]], axis=1)


def q_latent(q, w_uk2, heads, nope, row_start, n_rows):
    m = n_rows
    assert row_start % m == 0
    first = row_start // m
    hw = q.shape[1] // heads
    rank = w_uk2.shape[0]
    return pl.pallas_call(
        functools.partial(_qlat_kernel, nope=nope),
        grid=(heads,),
        in_specs=[pl.BlockSpec((m, hw), lambda h: (first, h)),
                  pl.BlockSpec((rank, nope), lambda h: (0, h))],
        out_specs=pl.BlockSpec((1, m, rank + hw - nope), lambda h: (h, 0, 0)),
        out_shape=jax.ShapeDtypeStruct((heads, m, rank + hw - nope), BF16),
        compiler_params=_params("arbitrary"),
        name="q_latent",
    )(q, w_uk2)


def _decode_kernel(pt_ref, q_ref, new_ref, ck_hbm, kr_hbm, o_ref, ck_buf, kr_buf, sems,
                   kc_sc, krt_sc, m_sc, l_sc, acc_sc, *, pages, c_exp, rank, rope, ts):
    b, j = pl.program_id(0), pl.program_id(1)
    nb, nj = pl.num_programs(0), pl.num_programs(1)
    psz = ck_buf.shape[2]
    step = b * nj + j
    slot = lax.rem(step, 2)

    def page_copies(bb, jj, sl):
        copies = []
        for p in range(pages):
            page = pt_ref[bb, jj * pages + p]
            copies.append(pltpu.make_async_copy(ck_hbm.at[page], ck_buf.at[sl, p], sems.at[0, sl]))
            copies.append(pltpu.make_async_copy(kr_hbm.at[page], kr_buf.at[sl, p], sems.at[1, sl]))
        return copies

    def start_all(copies):
        for c in copies:
            c.start()

    @pl.when(step == 0)
    def _():
        start_all(page_copies(b, j, slot))

    @pl.when(step + 1 < nb * nj)
    def _():
        last = j == nj - 1
        start_all(page_copies(jnp.where(last, b + 1, b), jnp.where(last, 0, j + 1), 1 - slot))

    for c in page_copies(b, j, slot):
        c.wait()

    @pl.when(j == 0)
    def _():
        m_sc[...] = jnp.full_like(m_sc, -jnp.inf)
        l_sc[...] = jnp.zeros_like(l_sc)
        acc_sc[...] = jnp.zeros_like(acc_sc)

    for p in range(pages):
        kc_sc[p * psz:(p + 1) * psz, :] = ck_buf[slot, p].astype(BF16)
        krt_sc[:, p * psz:(p + 1) * psz] = kr_buf[slot, p].astype(BF16)

    q = q_ref[0]
    s = (lax.dot_general(q[:, :rank], kc_sc[...], NT, preferred_element_type=F32)
         + jnp.dot(q[:, rank:rank + rope], krt_sc[...], preferred_element_type=F32))
    m, alpha, p = _online_softmax(s, m_sc[...], c_exp)
    l = alpha * l_sc[...] + jnp.sum(p, axis=1, keepdims=True)
    acc = alpha * acc_sc[...] + jnp.dot(p.astype(BF16), kc_sc[...], preferred_element_type=F32)
    m_sc[...] = m
    l_sc[...] = l
    acc_sc[...] = acc

    @pl.when(j == pl.num_programs(1) - 1)
    def _():
        new = new_ref[0]
        s2 = lax.dot_general(q, new, NT, preferred_element_type=F32)
        qrow = lax.broadcasted_iota(jnp.int32, s2.shape, 0)
        heads = q.shape[0] // ts
        tok = sum((qrow >= t * heads).astype(jnp.int32) for t in range(1, ts))
        key = lax.broadcasted_iota(jnp.int32, s2.shape, 1)
        s2 = jnp.where(key <= tok, s2, -jnp.inf)
        _, alpha, p = _online_softmax(s2, m, c_exp)
        l2 = alpha * l + jnp.sum(p, axis=1, keepdims=True)
        acc2 = alpha * acc + jnp.dot(p.astype(BF16), new[:, :rank], preferred_element_type=F32)
        o_ref[0] = acc2 / l2


def attn_decode(q_full, new_keys, cache_ckv, cache_krope_t, page_table, scale, ts, pages):
    nb, rows, width = q_full.shape
    _, psz, rank = cache_ckv.shape
    rope = cache_krope_t.shape[1]
    n_pages = page_table.shape[1]
    nj = n_pages // pages

    grid_spec = pltpu.PrefetchScalarGridSpec(
        num_scalar_prefetch=1,
        grid=(nb, nj),
        in_specs=[pl.BlockSpec((1, rows, width), lambda b, j, pt: (b, 0, 0)),
                  pl.BlockSpec((1, new_keys.shape[1], width), lambda b, j, pt: (b, 0, 0)),
                  pl.BlockSpec(memory_space=pl.ANY),
                  pl.BlockSpec(memory_space=pl.ANY)],
        out_specs=pl.BlockSpec((1, rows, rank), lambda b, j, pt: (b, 0, 0)),
        scratch_shapes=[pltpu.VMEM((2, pages, psz, rank), cache_ckv.dtype),
                        pltpu.VMEM((2, pages, rope, psz), cache_krope_t.dtype),
                        pltpu.SemaphoreType.DMA((2, 2)),
                        pltpu.VMEM((pages * psz, rank), BF16),
                        pltpu.VMEM((rope, pages * psz), BF16),
                        pltpu.VMEM((rows, 1), F32),
                        pltpu.VMEM((rows, 1), F32),
                        pltpu.VMEM((rows, rank), F32)],
    )
    return pl.pallas_call(
        functools.partial(_decode_kernel, pages=pages, c_exp=scale * LOG2E, rank=rank,
                          rope=rope, ts=ts),
        grid_spec=grid_spec,
        out_shape=jax.ShapeDtypeStruct((nb, rows, rank), F32),
        compiler_params=_params("arbitrary", "arbitrary"),
        name="attn_decode",
    )(page_table, q_full, new_keys, cache_ckv, cache_krope_t)


def _oup_kernel(o_ref, w_ref, prev_ref, out_ref):
    del prev_ref
    out_ref[...] = jnp.dot(o_ref[0], w_ref[...].astype(BF16),
                           preferred_element_type=F32).astype(out_ref.dtype)


def o_up(o_lat, w_uv2, vdim, o_buf, row_start):
    heads, m, rank = o_lat.shape
    assert row_start % m == 0
    first = row_start // m
    return pl.pallas_call(
        _oup_kernel,
        grid=(heads,),
        in_specs=[pl.BlockSpec((1, m, rank), lambda h: (h, 0, 0)),
                  pl.BlockSpec((rank, vdim), lambda h: (0, h)),
                  pl.BlockSpec(memory_space=pl.ANY)],
        out_specs=pl.BlockSpec((m, vdim), lambda h: (first, h)),
        out_shape=jax.ShapeDtypeStruct(o_buf.shape, o_buf.dtype),
        input_output_aliases={2: 0},
        compiler_params=_params("arbitrary"),
        name="o_up",
    )(o_lat, w_uv2, o_buf)


def _rope_tables(pos, rope_dim, nope_pad, width):
    half = rope_dim // 2
    freqs = ROPE_THETA ** (-jnp.arange(half, dtype=F32) / half)
    ang = pos.astype(F32)[:, None] * freqs[None, :]
    cos, sin = jnp.cos(ang), jnp.sin(ang)
    n = pos.shape[0]
    zeros = lambda w: jnp.zeros((n, w), F32)
    tail = width - nope_pad - rope_dim
    ta = jnp.concatenate([jnp.ones((n, nope_pad), F32), cos, cos, zeros(tail)], axis=1)
    tb = jnp.concatenate([zeros(nope_pad + half), sin, zeros(tail)], axis=1)
    tc = jnp.concatenate([zeros(nope_pad), -sin, zeros(half + tail)], axis=1)
    return ta, tb, tc


def kernel(x_prompt, x_sample, cache_ckv, cache_krope, page_table, state_C, state_n, state_m,
           g_mix, g_ffn, w_ffn_in, w_ffn_out, w_a_in, b_a_gates, g_a_head, w_a_out,
           g_kv_in, w_kv_down, g_kv_latent, w_uk, w_uv, w_q_down, g_q_latent, w_q_up,
           w_b_out, g_final):
    bp, tp, d = x_prompt.shape
    bs, ts, _ = x_sample.shape
    mp, ms = bp * tp, bs * ts
    n_a = state_C.shape[0]
    n_layers = g_mix.shape[0]
    assert n_a == 1 and n_layers == 2, "one mLSTM layer followed by one MLA layer"
    _, _, heads_a, dk, dv = state_C.shape
    hk, hv = heads_a * dk, heads_a * dv
    rank, heads_b, nope = w_uk.shape
    vdim = w_uv.shape[2]
    rope = cache_krope.shape[2]
    psz = cache_ckv.shape[1]
    past = page_table.shape[1] * psz
    hw = nope + 2 * rope
    assert nope % LANES_V7X == 0 and hw % LANES_V7X == 0 and (2 * rope) % LANES_V7X == 0
    attn_scale = (nope + rope) ** -0.5

    x = jnp.concatenate([x_prompt.reshape(mp, d), x_sample.reshape(ms, d)], axis=0)
    pos = jnp.concatenate([jnp.tile(jnp.arange(tp), bp), jnp.tile(past + jnp.arange(ts), bs)])
    tabs = _rope_tables(pos, rope, 0, 2 * rope)

    (xn,) = rmsnorm_rows(x, g_mix[0:1])
    w_in_t = jnp.swapaxes(w_a_in[0], 0, 1)
    proj = matmul(xn, w_in_t, n_out=2 * hk + 2 * hv, out_dtype=BF16, w_t=True, tn_max=1024,
                  name="mlstm_in_proj")
    gates = gates_t(xn, w_in_t[2 * hk + 2 * hv:])
    bias = b_a_gates[0].reshape(2 * heads_a, 1)
    g_head = g_a_head[0].reshape(1, hv)

    hc = jnp.zeros((mp + ms, hv), BF16)
    hc, c_p, n_p, m_p = mlstm_prompt(proj, gates, bias, g_head, hc, bp, tp, heads_a, dk, dv,
                                     chunk=min(512, tp))
    group = 8
    rows = group * ts
    gates_blk = gates[:, mp:].reshape(2 * heads_a, bs // group, rows).transpose(1, 0, 2)
    m0_blk = jnp.repeat(state_m[0], ts, axis=0).reshape(bs // group, rows, heads_a)
    hc, c_s, n_s, m_s = mlstm_sample(proj, gates_blk, bias, g_head, m0_blk, state_n[0],
                                     state_C[0], hc, mp, heads_a, dk, dv, ts, group)
    x, xn = proj_res_norm(hc, w_a_out[0], x, g_ffn[0:1], name="mlstm_out_proj")

    x = matmul(swiglu_in(xn, w_ffn_in, 0), w_ffn_out, layer=0, res=x, w_buffers=1, name="ffn_out")

    xn_kv, xn_q = rmsnorm_rows(x, jnp.stack([g_kv_in, g_mix[1]]))
    w_kv_pad = jnp.pad(w_kv_down, ((0, 0), (0, rope)))
    g_lat = g_kv_latent.reshape(1, rank)
    ckv_p, ckvb_p, kr_p, krb_p = kv_down(xn_kv, w_kv_pad, g_lat, tabs, rank, rope, 0, mp)
    ckv_s, ckvb_s, kr_s, krb_s = kv_down(xn_kv, w_kv_pad, g_lat, tabs, rank, rope, mp, ms)

    xq = q_down(xn_q, w_q_down[0], g_q_latent[0].reshape(1, rank))
    w_qup_pad = jnp.pad(w_q_up[0].reshape(rank, heads_b, nope + rope),
                        ((0, 0), (0, 0), (0, rope))).reshape(rank, heads_b * hw)
    q = q_up(xq, w_qup_pad, tabs, hw, hw - nope)

    w_uk2 = w_uk.reshape(rank, heads_b * nope)
    w_uv2 = w_uv.reshape(rank, heads_b * vdim)
    kv = matmul(ckvb_p, jnp.concatenate([w_uk2, w_uv2], axis=1), out_dtype=BF16, tn_max=1024,
                name="kv_up")
    tq = min(2048, tp)
    o = jnp.zeros((mp + ms, heads_b * vdim), BF16)
    o = attn_prompt(q, kv, krb_p, o, bp, tp, heads_b, nope, vdim, attn_scale,
                    tq=tq, parts=max(1, tq // 256))

    q_lat = q_latent(q, w_uk2, heads_b, nope, mp, ms)
    width = rank + 2 * rope
    q_full = q_lat.reshape(heads_b, bs, ts, width).transpose(1, 2, 0, 3).reshape(bs, ts * heads_b, width)
    new_keys = jnp.concatenate([ckvb_s, krb_s], axis=1).reshape(bs, ts, width)
    new_keys = jnp.pad(new_keys, ((0, 0), (0, BF16_SUBLANES_V7X - ts), (0, 0)))
    cache_krope_t = jnp.swapaxes(cache_krope, 1, 2)
    o_lat = attn_decode(q_full, new_keys, cache_ckv, cache_krope_t, page_table, attn_scale, ts,
                        pages=min(64, page_table.shape[1]))
    o_lat = o_lat.reshape(bs, ts, heads_b, rank).transpose(2, 0, 1, 3).reshape(heads_b, ms, rank)
    o = o_up(o_lat.astype(BF16), w_uv2, vdim, o, mp)

    x, xn = proj_res_norm(o, w_b_out[0], x, g_ffn[1:2], name="mla_out_proj")

    x = matmul(swiglu_in(xn, w_ffn_in, 1), w_ffn_out, layer=1, res=x, w_buffers=1, name="ffn_out")
    g_fin = g_final.reshape(1, d)
    (y_p,) = rmsnorm_rows(x, g_fin, out_dtype=F32, row_start=0, n_rows=mp)
    (y_s,) = rmsnorm_rows(x, g_fin, out_dtype=F32, row_start=mp, n_rows=ms)

    return (y_p.reshape(bp, tp, d), y_s.reshape(bs, ts, d),
            c_p[None], n_p[None], m_p[None, :, :, 0],
            ckv_p.reshape(bp, tp, rank), kr_p.reshape(bp, tp, rope),
            c_s[None], n_s[None], m_s[None, :, :, 0],
            ckv_s.reshape(bs, ts, rank), kr_s.reshape(bs, ts, rope))
```

```python
import functools

import jax
import jax.numpy as jnp
from jax import lax
from jax.experimental import pallas as pl
from jax.experimental.pallas import tpu as pltpu

F32 = jnp.float32
BF16 = jnp.bfloat16

EPS = 1e-6
GATE_CAP = 15.0
ROPE_THETA = 10000.0
LOG2E = 1.4426950408889634

LANES_V7X = 128
BF16_SUBLANES_V7X = 16
VMEM_BYTES_V7X = 64 * 1024 * 1024
VMEM_BUDGET = VMEM_BYTES_V7X * 5 // 8
VMEM_LIMIT = VMEM_BYTES_V7X * 7 // 8

NT = (((1,), (1,)), ((), ()))
TN = (((0,), (0,)), ((), ()))


def _params(*sem):
    return pltpu.CompilerParams(dimension_semantics=sem, vmem_limit_bytes=VMEM_LIMIT)


def _largest_divisor(n, cap, mult):
    best = None
    for d in range(mult, min(n, cap) + 1, mult):
        if n % d == 0:
            best = d
    return n if best is None else best


def _row_tile(m, cap=1088):
    return _largest_divisor(m, cap, BF16_SUBLANES_V7X)


def _rms_kernel(x_ref, g_ref, *o_refs):
    x = x_ref[...]
    y = x * lax.rsqrt(jnp.mean(x * x, axis=-1, keepdims=True) + EPS)
    for i, o_ref in enumerate(o_refs):
        o_ref[...] = (y * g_ref[i:i + 1, :]).astype(o_ref.dtype)


def rmsnorm_rows(x, gains, out_dtype=BF16, row_start=0, n_rows=None):
    d = x.shape[1]
    m = x.shape[0] if n_rows is None else n_rows
    n = gains.shape[0]
    tm = _largest_divisor(m, 512, BF16_SUBLANES_V7X)
    assert row_start % tm == 0
    first = row_start // tm
    outs = pl.pallas_call(
        _rms_kernel,
        grid=(m // tm,),
        in_specs=[pl.BlockSpec((tm, d), lambda i: (i + first, 0)),
                  pl.BlockSpec((n, d), lambda i: (0, 0))],
        out_specs=[pl.BlockSpec((tm, d), lambda i: (i, 0))] * n,
        out_shape=[jax.ShapeDtypeStruct((m, d), out_dtype)] * n,
        compiler_params=_params("arbitrary"),
        name="rmsnorm_rows",
    )(x, gains)
    return outs


def _mm_tiles(m, k, n, out_bytes, n_w=1, has_res=False, w_buffers=2, tn_max=512):
    best = None
    for tn in (1024, 512, 256, 128):
        if n % tn or tn > tn_max:
            continue
        for tm in range(BF16_SUBLANES_V7X, min(m, 1088) + 1, BF16_SUBLANES_V7X):
            if m % tm:
                continue
            need = (2 * tm * k * 2 + n_w * (w_buffers * k * tn * 4 + k * tn * 2)
                    + 2 * tm * tn * out_bytes + (2 * tm * tn * 4 if has_res else 0))
            if need <= VMEM_BUDGET and (best is None or tm * tn > best[0] * best[1]):
                best = (tm, tn)
    assert best is not None, (m, k, n)
    return best


def _mm_kernel(a_ref, w_ref, *rest, has_res, w_t):
    if has_res:
        r_ref, o_ref, w_sc = rest
    else:
        o_ref, w_sc = rest

    @pl.when(pl.program_id(1) == 0)
    def _():
        w_sc[...] = w_ref[...].astype(BF16)

    if w_t:
        acc = lax.dot_general(a_ref[...], w_sc[...], NT, preferred_element_type=F32)
    else:
        acc = jnp.dot(a_ref[...], w_sc[...], preferred_element_type=F32)
    if has_res:
        acc = acc + r_ref[...]
    o_ref[...] = acc.astype(o_ref.dtype)


def _layer_spec(block, index_map, layer, buffers=2):
    mode = {} if buffers == 2 else {"pipeline_mode": pl.Buffered(buffers)}
    if layer is None:
        return pl.BlockSpec(block, index_map, **mode)
    return pl.BlockSpec((None,) + block, lambda *g: (layer,) + index_map(*g), **mode)


def matmul(a, w, n_out=None, res=None, out_dtype=F32, w_t=False, layer=None, w_buffers=2,
           tn_max=512, name="matmul"):
    m, k = a.shape
    w_shape = w.shape if layer is None else w.shape[1:]
    n = (w_shape[0] if w_t else w_shape[1]) if n_out is None else n_out
    tm, tn = _mm_tiles(m, k, n, jnp.dtype(out_dtype).itemsize, has_res=res is not None,
                       w_buffers=w_buffers, tn_max=tn_max)
    w_block = (tn, k) if w_t else (k, tn)
    w_map = (lambda j, i: (j, 0)) if w_t else (lambda j, i: (0, j))
    in_specs = [pl.BlockSpec((tm, k), lambda j, i: (i, 0)),
                _layer_spec(w_block, w_map, layer, w_buffers)]
    args = [a, w]
    if res is not None:
        in_specs.append(pl.BlockSpec((tm, tn), lambda j, i: (i, j)))
        args.append(res)
    return pl.pallas_call(
        functools.partial(_mm_kernel, has_res=res is not None, w_t=w_t),
        grid=(n // tn, m // tm),
        in_specs=in_specs,
        out_specs=pl.BlockSpec((tm, tn), lambda j, i: (i, j)),
        out_shape=jax.ShapeDtypeStruct((m, n), out_dtype),
        scratch_shapes=[pltpu.VMEM(w_block, BF16)],
        compiler_params=_params("arbitrary", "arbitrary"),
        name=name,
    )(*args)


def _proj_norm_kernel(a_ref, w_ref, r_ref, g_ref, x_ref, *rest):
    xn_refs, w_sc = rest[:-1], rest[-1]

    @pl.when(pl.program_id(0) == 0)
    def _():
        w_sc[...] = w_ref[...].astype(BF16)

    x = r_ref[...] + jnp.dot(a_ref[...], w_sc[...], preferred_element_type=F32)
    x_ref[...] = x
    y = x * lax.rsqrt(jnp.mean(x * x, axis=-1, keepdims=True) + EPS)
    for i, o_ref in enumerate(xn_refs):
        o_ref[...] = (y * g_ref[i:i + 1, :]).astype(o_ref.dtype)


def proj_res_norm(a, w, res, gains, name):
    m, k = a.shape
    n = w.shape[1]
    ng = gains.shape[0]
    tm = _row_tile(m, 272)
    row = lambda i: (i, 0)
    outs = pl.pallas_call(
        _proj_norm_kernel,
        grid=(m // tm,),
        in_specs=[pl.BlockSpec((tm, k), row),
                  pl.BlockSpec((k, n), lambda i: (0, 0), pipeline_mode=pl.Buffered(1)),
                  pl.BlockSpec((tm, n), row),
                  pl.BlockSpec((ng, n), lambda i: (0, 0))],
        out_specs=[pl.BlockSpec((tm, n), row)] * (1 + ng),
        out_shape=[jax.ShapeDtypeStruct((m, n), F32)] + [jax.ShapeDtypeStruct((m, n), BF16)] * ng,
        scratch_shapes=[pltpu.VMEM((k, n), BF16)],
        compiler_params=_params("arbitrary"),
        name=name,
    )(a, w, res, gains)
    return outs


def _swiglu_kernel(a_ref, wg_ref, wu_ref, o_ref, wg_sc, wu_sc):
    @pl.when(pl.program_id(1) == 0)
    def _():
        wg_sc[...] = wg_ref[...].astype(BF16)
        wu_sc[...] = wu_ref[...].astype(BF16)

    a = a_ref[...]
    g = jnp.dot(a, wg_sc[...], preferred_element_type=F32)
    u = jnp.dot(a, wu_sc[...], preferred_element_type=F32)
    o_ref[...] = (g * jax.nn.sigmoid(g) * u).astype(o_ref.dtype)


def swiglu_in(a, w_in, layer):
    m, k = a.shape
    f = w_in.shape[2] // 2
    tm, tf = _mm_tiles(m, k, f, 2, n_w=2)
    nf = f // tf
    return pl.pallas_call(
        _swiglu_kernel,
        grid=(nf, m // tm),
        in_specs=[pl.BlockSpec((tm, k), lambda j, i: (i, 0)),
                  _layer_spec((k, tf), lambda j, i: (0, j), layer),
                  _layer_spec((k, tf), lambda j, i: (0, j + nf), layer)],
        out_specs=pl.BlockSpec((tm, tf), lambda j, i: (i, j)),
        out_shape=jax.ShapeDtypeStruct((m, f), BF16),
        scratch_shapes=[pltpu.VMEM((k, tf), BF16), pltpu.VMEM((k, tf), BF16)],
        compiler_params=_params("arbitrary", "arbitrary"),
        name="swiglu_in",
    )(a, w_in, w_in)


def _gates_kernel(wt_ref, a_ref, o_ref):
    o_ref[...] = lax.dot_general(wt_ref[...].astype(BF16), a_ref[...], NT,
                                 preferred_element_type=F32)


def gates_t(a, w_gate_t):
    m, k = a.shape
    g = w_gate_t.shape[0]
    tm = _largest_divisor(m, 512, LANES_V7X)
    return pl.pallas_call(
        _gates_kernel,
        grid=(m // tm,),
        in_specs=[pl.BlockSpec((g, k), lambda i: (0, 0)),
                  pl.BlockSpec((tm, k), lambda i: (i, 0))],
        out_specs=pl.BlockSpec((g, tm), lambda i: (0, i)),
        out_shape=jax.ShapeDtypeStruct((g, m), F32),
        compiler_params=_params("arbitrary"),
        name="gates_t",
    )(w_gate_t, a)


def _rope_lanes(x, a, b, c):
    w = x.shape[1]
    return x * a + pltpu.roll(x, 32, 1) * b + pltpu.roll(x, w - 32, 1) * c


def _rmsnorm_bf16(x, g):
    return (x * lax.rsqrt(jnp.mean(x * x, axis=-1, keepdims=True) + EPS) * g).astype(BF16)


def _qdown_kernel(x_ref, gi_ref, w_ref, g_ref, o_ref, w_sc):
    @pl.when(pl.program_id(0) == 0)
    def _():
        w_sc[...] = w_ref[...].astype(BF16)

    y = jnp.dot(_rmsnorm_bf16(x_ref[...], gi_ref[...]), w_sc[...], preferred_element_type=F32)
    y = y * lax.rsqrt(jnp.mean(y * y, axis=-1, keepdims=True) + EPS)
    o_ref[...] = (y * g_ref[...]).astype(o_ref.dtype)


def q_down(x, g_in, w, g):
    m, k = x.shape
    r = w.shape[1]
    tm = _row_tile(m)
    return pl.pallas_call(
        _qdown_kernel,
        grid=(m // tm,),
        in_specs=[pl.BlockSpec((tm, k), lambda i: (i, 0)),
                  pl.BlockSpec((1, k), lambda i: (0, 0)),
                  pl.BlockSpec((k, r), lambda i: (0, 0)),
                  pl.BlockSpec((1, r), lambda i: (0, 0))],
        out_specs=pl.BlockSpec((tm, r), lambda i: (i, 0)),
        out_shape=jax.ShapeDtypeStruct((m, r), BF16),
        scratch_shapes=[pltpu.VMEM((k, r), BF16)],
        compiler_params=_params("arbitrary"),
        name="q_down",
    )(x, g_in, w, g)


def _qup_kernel(a_ref, w_ref, ta_ref, tb_ref, tc_ref, o_ref, w_sc, *, hw):
    @pl.when(pl.program_id(0) == 0)
    def _():
        w_sc[...] = w_ref[...].astype(BF16)

    a = a_ref[...]
    ta, tb, tc = ta_ref[...], tb_ref[...], tc_ref[...]
    rw = ta.shape[1]
    for h in range(w_sc.shape[1] // hw):
        y = jnp.dot(a, w_sc[:, h * hw:(h + 1) * hw], preferred_element_type=F32)
        o_ref[:, h * hw:(h + 1) * hw - rw] = y[:, :hw - rw].astype(o_ref.dtype)
        o_ref[:, (h + 1) * hw - rw:(h + 1) * hw] = _rope_lanes(y[:, hw - rw:], ta, tb, tc).astype(o_ref.dtype)


def q_up(a, w_pad, tabs, hw, rw):
    m, k = a.shape
    n = w_pad.shape[1]
    tm = _row_tile(m, 544)
    assert tabs[0].shape[1] == rw
    tspec = pl.BlockSpec((tm, rw), lambda i: (i, 0))
    return pl.pallas_call(
        functools.partial(_qup_kernel, hw=hw),
        grid=(m // tm,),
        in_specs=[pl.BlockSpec((tm, k), lambda i: (i, 0)),
                  pl.BlockSpec((k, n), lambda i: (0, 0)),
                  tspec, tspec, tspec],
        out_specs=pl.BlockSpec((tm, n), lambda i: (i, 0)),
        out_shape=jax.ShapeDtypeStruct((m, n), BF16),
        scratch_shapes=[pltpu.VMEM((k, n), BF16)],
        compiler_params=_params("arbitrary"),
        name="q_up",
    )(a, w_pad, *tabs)


def _kvdown_kernel(x_ref, gi_ref, w_ref, g_ref, ta_ref, tb_ref, tc_ref,
                   ckv_ref, ckvb_ref, kr_ref, krb_ref, w_sc, *, rank):
    @pl.when(pl.program_id(0) == 0)
    def _():
        w_sc[...] = w_ref[...].astype(BF16)

    y = jnp.dot(_rmsnorm_bf16(x_ref[...], gi_ref[...]), w_sc[...], preferred_element_type=F32)
    lat = y[:, :rank]
    lat = lat * lax.rsqrt(jnp.mean(lat * lat, axis=-1, keepdims=True) + EPS) * g_ref[...]
    ckv_ref[...] = lat
    ckvb_ref[...] = lat.astype(BF16)
    kr = _rope_lanes(y[:, rank:], ta_ref[...], tb_ref[...], tc_ref[...])
    kr_ref[...] = kr[:, :kr_ref.shape[1]]
    krb_ref[...] = kr.astype(BF16)


def kv_down(x, g_in, w_pad, g, tabs, rank, rope, row_start, n_rows):
    k = x.shape[1]
    m = n_rows
    n = w_pad.shape[1]
    rw = n - rank
    tm = _row_tile(m)
    assert row_start % tm == 0
    first = row_start // tm
    assert tabs[0].shape[1] == rw
    tspec = pl.BlockSpec((tm, rw), lambda i: (i + first, 0))
    return pl.pallas_call(
        functools.partial(_kvdown_kernel, rank=rank),
        grid=(m // tm,),
        in_specs=[pl.BlockSpec((tm, k), lambda i: (i + first, 0)),
                  pl.BlockSpec((1, k), lambda i: (0, 0)),
                  pl.BlockSpec((k, n), lambda i: (0, 0)),
                  pl.BlockSpec((1, rank), lambda i: (0, 0)),
                  tspec, tspec, tspec],
        out_specs=[pl.BlockSpec((tm, rank), lambda i: (i, 0)),
                   pl.BlockSpec((tm, rank), lambda i: (i, 0)),
                   pl.BlockSpec((tm, rope), lambda i: (i, 0)),
                   pl.BlockSpec((tm, rw), lambda i: (i, 0))],
        out_shape=[jax.ShapeDtypeStruct((m, rank), F32),
                   jax.ShapeDtypeStruct((m, rank), BF16),
                   jax.ShapeDtypeStruct((m, rope), F32),
                   jax.ShapeDtypeStruct((m, rw), BF16)],
        scratch_shapes=[pltpu.VMEM((k, n), BF16)],
        compiler_params=_params("arbitrary"),
        name="kv_down",
    )(x, g_in, w_pad, g, *tabs)


def _split3(x):
    hi = x.astype(BF16)
    r = x - hi.astype(F32)
    mid = r.astype(BF16)
    lo = (r - mid.astype(F32)).astype(BF16)
    return hi, mid, lo


def _dot_exact(x, m01):
    return sum(jnp.dot(p, m01, preferred_element_type=F32) for p in _split3(x))


def _transpose_exact(eye, x):
    return sum(lax.dot_general(eye, p, NT, preferred_element_type=F32) for p in _split3(x))


def _log_sigmoid(x):
    return jnp.minimum(x, 0.0) - jnp.log1p(jnp.exp(-jnp.abs(x)))


def _softcap(z):
    return GATE_CAP * jnp.tanh(z / GATE_CAP)


def _mlstm_prompt_kernel(q_ref, k_ref, v_ref, o_ref, gt_ref, bias_ref, gh_ref, hc_prev_ref,
                         hc_ref, c_out, n_out, m_out, c_sc, n_sc, m_sc, *, heads, dk, dv):
    del hc_prev_ref
    c = pl.program_id(1)
    L = q_ref.shape[0]
    scale = dk ** -0.5

    @pl.when(c == 0)
    def _():
        c_sc[...] = jnp.zeros_like(c_sc)
        n_sc[...] = jnp.zeros_like(n_sc)
        m_sc[...] = jnp.zeros_like(m_sc)

    z = _softcap(gt_ref[...] + bias_ref[...])
    li_r = z[:heads]
    lf_r = _log_sigmoid(z[heads:])
    row = lax.broadcasted_iota(jnp.int32, (L, L), 0)
    col = lax.broadcasted_iota(jnp.int32, (L, L), 1)
    causal = col <= row
    upper = (row <= col).astype(BF16)
    eye = (row == col).astype(BF16)
    b_r = _dot_exact(lf_r, upper)
    cols = _transpose_exact(eye, jnp.concatenate([li_r, b_r], axis=0))

    for h in range(heads):
        q = q_ref[:, h * dk:(h + 1) * dk]
        k = k_ref[:, h * dk:(h + 1) * dk]
        v = v_ref[:, h * dv:(h + 1) * dv]
        li_c = cols[:, h:h + 1]
        b_c = cols[:, heads + h:heads + h + 1]
        m_prev = m_sc[h:h + 1, 0:1]
        n_prev = n_sc[h:h + 1, :]
        c_prev = c_sc[h]

        logw = jnp.where(causal, b_c - b_r[h:h + 1, :] + li_r[h:h + 1, :], -jnp.inf)
        log_inter = b_c + m_prev
        m_t = jnp.maximum(log_inter, jnp.max(logw, axis=1, keepdims=True))
        s = lax.dot_general(q, k, NT, preferred_element_type=F32) * scale * jnp.exp(logw - m_t)
        w_inter = jnp.exp(log_inter - m_t)
        num = (jnp.dot(s.astype(BF16), v, preferred_element_type=F32)
               + w_inter * jnp.dot(q, c_prev.astype(BF16), preferred_element_type=F32))
        qn = jnp.sum(q.astype(F32) * n_prev, axis=1, keepdims=True)
        den = jnp.sum(s, axis=1, keepdims=True) + w_inter * qn
        hh = num / jnp.maximum(jnp.abs(den), jnp.exp(-m_t))
        hh = hh * lax.rsqrt(jnp.mean(hh * hh, axis=1, keepdims=True) + EPS)
        og = o_ref[:, h * dv:(h + 1) * dv].astype(F32)
        hc_ref[:, h * dv:(h + 1) * dv] = (
            hh * gh_ref[:, h * dv:(h + 1) * dv] * jax.nn.sigmoid(og)).astype(hc_ref.dtype)

        m_end = m_t[L - 1:L, :]
        b_last = b_c[L - 1:L, :]
        w_end = jnp.exp(b_last - b_c + li_c - m_end)
        decay = jnp.exp(b_last + m_prev - m_end)
        kw = k.astype(F32) * (scale * w_end)
        c_sc[h] = decay * c_prev + lax.dot_general(kw.astype(BF16), v, TN,
                                                   preferred_element_type=F32)
        n_sc[h:h + 1, :] = decay * n_prev + jnp.sum(kw, axis=0, keepdims=True)
        m_sc[h:h + 1, :] = jnp.broadcast_to(m_end, (1, m_sc.shape[1]))

    @pl.when(c == pl.num_programs(1) - 1)
    def _():
        c_out[0] = c_sc[...]
        n_out[0] = n_sc[...]
        m_out[0] = m_sc[...]


def mlstm_prompt(proj, gates, bias, g_head, hc_buf, batch, seq, heads, dk, dv, chunk):
    hk, hv = heads * dk, heads * dv
    nc = seq // chunk
    row = lambda b, c: b * nc + c
    return pl.pallas_call(
        functools.partial(_mlstm_prompt_kernel, heads=heads, dk=dk, dv=dv),
        grid=(batch, nc),
        in_specs=[pl.BlockSpec((chunk, hk), lambda b, c: (row(b, c), 0)),
                  pl.BlockSpec((chunk, hk), lambda b, c: (row(b, c), 1)),
                  pl.BlockSpec((chunk, hv), lambda b, c: (row(b, c), 2 * hk // hv)),
                  pl.BlockSpec((chunk, hv), lambda b, c: (row(b, c), 2 * hk // hv + 1)),
                  pl.BlockSpec((2 * heads, chunk), lambda b, c: (0, row(b, c))),
                  pl.BlockSpec((2 * heads, 1), lambda b, c: (0, 0)),
                  pl.BlockSpec((1, hv), lambda b, c: (0, 0)),
                  pl.BlockSpec(memory_space=pl.ANY)],
        out_specs=[pl.BlockSpec((chunk, hv), lambda b, c: (row(b, c), 0)),
                   pl.BlockSpec((1, heads, dk, dv), lambda b, c: (b, 0, 0, 0)),
                   pl.BlockSpec((1, heads, dk), lambda b, c: (b, 0, 0)),
                   pl.BlockSpec((1, heads, LANES_V7X), lambda b, c: (b, 0, 0))],
        out_shape=[jax.ShapeDtypeStruct(hc_buf.shape, hc_buf.dtype),
                   jax.ShapeDtypeStruct((batch, heads, dk, dv), F32),
                   jax.ShapeDtypeStruct((batch, heads, dk), F32),
                   jax.ShapeDtypeStruct((batch, heads, LANES_V7X), F32)],
        scratch_shapes=[pltpu.VMEM((heads, dk, dv), F32),
                        pltpu.VMEM((heads, dk), F32),
                        pltpu.VMEM((heads, LANES_V7X), F32)],
        input_output_aliases={7: 0},
        compiler_params=_params("arbitrary", "arbitrary"),
        name="mlstm_prompt",
    )(proj, proj, proj, proj, gates, bias, g_head, hc_buf)


def _mlstm_sample_kernel(q_ref, k_ref, v_ref, o_ref, gt_ref, bias_ref, gh_ref, m0_ref,
                         n0_ref, c0_ref, hc_prev_ref, hc_ref, c_out, n_out, m_out,
                         *, heads, dk, dv, ts):
    del hc_prev_ref
    R = q_ref.shape[0]
    G = R // ts
    scale = dk ** -0.5

    z = _softcap(gt_ref[0] + bias_ref[...])
    li_r = z[:heads]
    lf_r = _log_sigmoid(z[heads:])
    row = lax.broadcasted_iota(jnp.int32, (R, R), 0)
    col = lax.broadcasted_iota(jnp.int32, (R, R), 1)
    seq_id = lambda i: sum((i >= g * ts).astype(jnp.int32) for g in range(1, G))
    same = seq_id(row) == seq_id(col)
    causal = same & (col <= row)
    upper = (same & (row <= col)).astype(BF16)
    eye = (row == col).astype(BF16)
    b_r = _dot_exact(lf_r, upper)
    cols = _transpose_exact(eye, jnp.concatenate([li_r, b_r], axis=0))
    seq_of_row = seq_id(lax.broadcasted_iota(jnp.int32, (R, 1), 0))
    m0 = m0_ref[0]

    for h in range(heads):
        q = q_ref[:, h * dk:(h + 1) * dk]
        k = k_ref[:, h * dk:(h + 1) * dk]
        v = v_ref[:, h * dv:(h + 1) * dv]
        li_c = cols[:, h:h + 1]
        b_c = cols[:, heads + h:heads + h + 1]
        m_prev = m0[:, h:h + 1]

        logw = jnp.where(causal, b_c - b_r[h:h + 1, :] + li_r[h:h + 1, :], -jnp.inf)
        log_inter = b_c + m_prev
        m_t = jnp.maximum(log_inter, jnp.max(logw, axis=1, keepdims=True))
        s = lax.dot_general(q, k, NT, preferred_element_type=F32) * scale * jnp.exp(logw - m_t)
        w_inter = jnp.exp(log_inter - m_t)

        qc = jnp.zeros((R, dv), F32)
        qn = jnp.zeros((R, 1), F32)
        qf = q.astype(F32)
        for g in range(G):
            mine = seq_of_row == g
            qc = jnp.where(mine, jnp.dot(q, c0_ref[g, h].astype(BF16),
                                         preferred_element_type=F32), qc)
            qn = jnp.where(mine, jnp.sum(qf * n0_ref[g, h:h + 1, :], axis=1, keepdims=True), qn)
        num = jnp.dot(s.astype(BF16), v, preferred_element_type=F32) + w_inter * qc
        den = jnp.sum(s, axis=1, keepdims=True) + w_inter * qn
        hh = num / jnp.maximum(jnp.abs(den), jnp.exp(-m_t))
        hh = hh * lax.rsqrt(jnp.mean(hh * hh, axis=1, keepdims=True) + EPS)
        og = o_ref[:, h * dv:(h + 1) * dv].astype(F32)
        hc_ref[:, h * dv:(h + 1) * dv] = (
            hh * gh_ref[:, h * dv:(h + 1) * dv] * jax.nn.sigmoid(og)).astype(hc_ref.dtype)

        kf = k.astype(F32)
        for g in range(G):
            last = (g + 1) * ts - 1
            m_end = m_t[last:last + 1, :]
            b_last = b_c[last:last + 1, :]
            mine = seq_of_row == g
            w_end = jnp.where(mine, jnp.exp(b_last - b_c + li_c - m_end), 0.0)
            decay = jnp.exp(b_last + m_prev[last:last + 1, :] - m_end)
            kw = kf * (scale * w_end)
            c_out[g, h] = decay * c0_ref[g, h] + lax.dot_general(
                kw.astype(BF16), v, TN, preferred_element_type=F32)
            n_out[g, h:h + 1, :] = decay * n0_ref[g, h:h + 1, :] + jnp.sum(kw, axis=0, keepdims=True)
            m_out[g, h:h + 1, :] = jnp.broadcast_to(m_end, (1, m_out.shape[2]))


def mlstm_sample(proj, gates_blk, bias, g_head, m0_blk, n0, c0, hc_buf, row_start,
                 heads, dk, dv, ts, group):
    hk, hv = heads * dk, heads * dv
    rows = group * ts
    nb = c0.shape[0]
    steps = nb // group
    assert row_start % rows == 0
    first = row_start // rows
    return pl.pallas_call(
        functools.partial(_mlstm_sample_kernel, heads=heads, dk=dk, dv=dv, ts=ts),
        grid=(steps,),
        in_specs=[pl.BlockSpec((rows, hk), lambda i: (i + first, 0)),
                  pl.BlockSpec((rows, hk), lambda i: (i + first, 1)),
                  pl.BlockSpec((rows, hv), lambda i: (i + first, 2 * hk // hv)),
                  pl.BlockSpec((rows, hv), lambda i: (i + first, 2 * hk // hv + 1)),
                  pl.BlockSpec((1, 2 * heads, rows), lambda i: (i, 0, 0)),
                  pl.BlockSpec((2 * heads, 1), lambda i: (0, 0)),
                  pl.BlockSpec((1, hv), lambda i: (0, 0)),
                  pl.BlockSpec((1, rows, heads), lambda i: (i, 0, 0)),
                  pl.BlockSpec((group, heads, dk), lambda i: (i, 0, 0)),
                  pl.BlockSpec((group, heads, dk, dv), lambda i: (i, 0, 0, 0)),
                  pl.BlockSpec(memory_space=pl.ANY)],
        out_specs=[pl.BlockSpec((rows, hv), lambda i: (i + first, 0)),
                   pl.BlockSpec((group, heads, dk, dv), lambda i: (i, 0, 0, 0)),
                   pl.BlockSpec((group, heads, dk), lambda i: (i, 0, 0)),
                   pl.BlockSpec((group, heads, LANES_V7X), lambda i: (i, 0, 0))],
        out_shape=[jax.ShapeDtypeStruct(hc_buf.shape, hc_buf.dtype),
                   jax.ShapeDtypeStruct((nb, heads, dk, dv), F32),
                   jax.ShapeDtypeStruct((nb, heads, dk), F32),
                   jax.ShapeDtypeStruct((nb, heads, LANES_V7X), F32)],
        input_output_aliases={10: 0},
        compiler_params=_params("arbitrary"),
        name="mlstm_sample",
    )(proj, proj, proj, proj, gates_blk, bias, g_head, m0_blk, n0, c0, hc_buf)


def _online_softmax(s, m_prev, c_exp):
    m_new = jnp.maximum(m_prev, jnp.max(s, axis=1, keepdims=True))
    alpha = jnp.exp2((m_prev - m_new) * c_exp)
    p = jnp.exp2((s - m_new) * c_exp)
    return m_new, alpha, p


def _attn_prompt_kernel(q_ref, kn_ref, kr_ref, v_ref, o_prev_ref, o_ref, kf_sc, vx_sc, m_sc,
                        acc_sc, *, c_exp, nope, vdim, parts):
    del o_prev_ref
    qi = pl.program_id(2)
    tq = q_ref.shape[0]
    tp = tq // parts

    @pl.when(qi == 0)
    def _():
        kf_sc[:, :nope] = kn_ref[...]
        kf_sc[:, nope:] = kr_ref[...]
        vx_sc[:, :vdim] = v_ref[...]
        lane = lax.broadcasted_iota(jnp.int32, (vx_sc.shape[0], vx_sc.shape[1] - vdim), 1)
        vx_sc[:, vdim:] = (lane == 0).astype(BF16)

    m_sc[...] = jnp.full_like(m_sc, -jnp.inf)
    acc_sc[...] = jnp.zeros_like(acc_sc)

    def update(part, k, vx, row_offset):
        s = lax.dot_general(q_ref[part * tp:(part + 1) * tp, :], k, NT, preferred_element_type=F32)
        if row_offset is not None:
            row = lax.broadcasted_iota(jnp.int32, s.shape, 0) + row_offset
            col = lax.broadcasted_iota(jnp.int32, s.shape, 1)
            s = jnp.where(col <= row, s, -jnp.inf)
        m_new, alpha, p = _online_softmax(s, m_sc[part], c_exp)
        acc_sc[part] = alpha * acc_sc[part] + jnp.dot(p.astype(BF16), vx,
                                                      preferred_element_type=F32)
        m_sc[part] = m_new

    def body(ki, carry):
        start = pl.multiple_of(ki * tq, tq)
        k = kf_sc[pl.ds(start, tq), :]
        vx = vx_sc[pl.ds(start, tq), :]
        for part in range(parts):
            update(part, k, vx, None)
        return carry

    lax.fori_loop(0, qi, body, 0)
    start = pl.multiple_of(qi * tq, tq)
    for part in range(parts):
        n_keys = (part + 1) * tp
        update(part, kf_sc[pl.ds(start, n_keys), :], vx_sc[pl.ds(start, n_keys), :], part * tp)
    for part in range(parts):
        acc = acc_sc[part]
        o_ref[part * tp:(part + 1) * tp, :] = (acc[:, :vdim] / acc[:, vdim:vdim + 1]).astype(o_ref.dtype)


def attn_prompt(q, kv, kr, o_buf, batch, seq, heads, nope, vdim, scale, tq, parts):
    hw = q.shape[1] // heads
    nq = seq // tq
    assert (heads * nope) % vdim == 0
    v_col0 = heads * nope // vdim
    return pl.pallas_call(
        functools.partial(_attn_prompt_kernel, c_exp=scale * LOG2E, nope=nope, vdim=vdim,
                          parts=parts),
        grid=(batch, heads, nq),
        in_specs=[pl.BlockSpec((tq, hw), lambda b, h, i: (b * nq + i, h)),
                  pl.BlockSpec((seq, nope), lambda b, h, i: (b, h)),
                  pl.BlockSpec((seq, hw - nope), lambda b, h, i: (b, 0)),
                  pl.BlockSpec((seq, vdim), lambda b, h, i: (b, v_col0 + h)),
                  pl.BlockSpec(memory_space=pl.ANY)],
        out_specs=pl.BlockSpec((tq, vdim), lambda b, h, i: (b * nq + i, h)),
        out_shape=jax.ShapeDtypeStruct(o_buf.shape, o_buf.dtype),
        input_output_aliases={4: 0},
        scratch_shapes=[pltpu.VMEM((seq, hw), BF16),
                        pltpu.VMEM((seq, 2 * vdim), BF16),
                        pltpu.VMEM((parts, tq // parts, 1), F32),
                        pltpu.VMEM((parts, tq // parts, 2 * vdim), F32)],
        compiler_params=_params("arbitrary", "arbitrary", "arbitrary"),
        name="attn_prompt",
    )(q, kv, kr, kv, o_buf)


def _qlat_kernel(q_ref, w_ref, o_ref, *, nope):
    q = q_ref[...]
    lat = lax.dot_general(q[:, :nope], w_ref[...].astype(BF16), NT, preferred_element_type=F32)
    o_ref[0] = jnp.concatenate([lat.astype(BF16), q[:, nope:]], axis=1)


def q_latent(q, w_uk2, heads, nope, row_start, n_rows):
    m = n_rows
    assert row_start % m == 0
    first = row_start // m
    hw = q.shape[1] // heads
    rank = w_uk2.shape[0]
    return pl.pallas_call(
        functools.partial(_qlat_kernel, nope=nope),
        grid=(heads,),
        in_specs=[pl.BlockSpec((m, hw), lambda h: (first, h)),
                  pl.BlockSpec((rank, nope), lambda h: (0, h))],
        out_specs=pl.BlockSpec((1, m, rank + hw - nope), lambda h: (h, 0, 0)),
        out_shape=jax.ShapeDtypeStruct((heads, m, rank + hw - nope), BF16),
        compiler_params=_params("arbitrary"),
        name="q_latent",
    )(q, w_uk2)


def _decode_kernel(pt_ref, q_ref, new_ref, ck_hbm, kr_hbm, o_ref, ck_buf, kr_buf, sems,
                   kc_sc, krt_sc, m_sc, l_sc, acc_sc, *, pages, c_exp, rank, rope, ts):
    b, j = pl.program_id(0), pl.program_id(1)
    nb, nj = pl.num_programs(0), pl.num_programs(1)
    psz = ck_buf.shape[2]
    step = b * nj + j
    slot = lax.rem(step, 2)

    def page_copies(bb, jj, sl):
        copies = []
        for p in range(pages):
            page = pt_ref[bb, jj * pages + p]
            copies.append(pltpu.make_async_copy(ck_hbm.at[page], ck_buf.at[sl, p], sems.at[0, sl]))
            copies.append(pltpu.make_async_copy(kr_hbm.at[page], kr_buf.at[sl, p], sems.at[1, sl]))
        return copies

    def start_all(copies):
        for c in copies:
            c.start()

    @pl.when(step == 0)
    def _():
        start_all(page_copies(b, j, slot))

    @pl.when(step + 1 < nb * nj)
    def _():
        last = j == nj - 1
        start_all(page_copies(jnp.where(last, b + 1, b), jnp.where(last, 0, j + 1), 1 - slot))

    for c in page_copies(b, j, slot):
        c.wait()

    @pl.when(j == 0)
    def _():
        m_sc[...] = jnp.full_like(m_sc, -jnp.inf)
        l_sc[...] = jnp.zeros_like(l_sc)
        acc_sc[...] = jnp.zeros_like(acc_sc)

    for p in range(pages):
        kc_sc[p * psz:(p + 1) * psz, :] = ck_buf[slot, p].astype(BF16)
        krt_sc[:, p * psz:(p + 1) * psz] = kr_buf[slot, p].astype(BF16)

    q = q_ref[0]
    s = (lax.dot_general(q[:, :rank], kc_sc[...], NT, preferred_element_type=F32)
         + jnp.dot(q[:, rank:rank + rope], krt_sc[...], preferred_element_type=F32))
    m, alpha, p = _online_softmax(s, m_sc[...], c_exp)
    l = alpha * l_sc[...] + jnp.sum(p, axis=1, keepdims=True)
    acc = alpha * acc_sc[...] + jnp.dot(p.astype(BF16), kc_sc[...], preferred_element_type=F32)
    m_sc[...] = m
    l_sc[...] = l
    acc_sc[...] = acc

    @pl.when(j == pl.num_programs(1) - 1)
    def _():
        new = new_ref[0]
        s2 = lax.dot_general(q, new, NT, preferred_element_type=F32)
        qrow = lax.broadcasted_iota(jnp.int32, s2.shape, 0)
        heads = q.shape[0] // ts
        tok = sum((qrow >= t * heads).astype(jnp.int32) for t in range(1, ts))
        key = lax.broadcasted_iota(jnp.int32, s2.shape, 1)
        s2 = jnp.where(key <= tok, s2, -jnp.inf)
        _, alpha, p = _online_softmax(s2, m, c_exp)
        l2 = alpha * l + jnp.sum(p, axis=1, keepdims=True)
        acc2 = alpha * acc + jnp.dot(p.astype(BF16), new[:, :rank], preferred_element_type=F32)
        o_ref[0] = acc2 / l2


def attn_decode(q_full, new_keys, cache_ckv, cache_krope_t, page_table, scale, ts, pages):
    nb, rows, width = q_full.shape
    _, psz, rank = cache_ckv.shape
    rope = cache_krope_t.shape[1]
    n_pages = page_table.shape[1]
    nj = n_pages // pages

    grid_spec = pltpu.PrefetchScalarGridSpec(
        num_scalar_prefetch=1,
        grid=(nb, nj),
        in_specs=[pl.BlockSpec((1, rows, width), lambda b, j, pt: (b, 0, 0)),
                  pl.BlockSpec((1, new_keys.shape[1], width), lambda b, j, pt: (b, 0, 0)),
                  pl.BlockSpec(memory_space=pl.ANY),
                  pl.BlockSpec(memory_space=pl.ANY)],
        out_specs=pl.BlockSpec((1, rows, rank), lambda b, j, pt: (b, 0, 0)),
        scratch_shapes=[pltpu.VMEM((2, pages, psz, rank), cache_ckv.dtype),
                        pltpu.VMEM((2, pages, rope, psz), cache_krope_t.dtype),
                        pltpu.SemaphoreType.DMA((2, 2)),
                        pltpu.VMEM((pages * psz, rank), BF16),
                        pltpu.VMEM((rope, pages * psz), BF16),
                        pltpu.VMEM((rows, 1), F32),
                        pltpu.VMEM((rows, 1), F32),
                        pltpu.VMEM((rows, rank), F32)],
    )
    return pl.pallas_call(
        functools.partial(_decode_kernel, pages=pages, c_exp=scale * LOG2E, rank=rank,
                          rope=rope, ts=ts),
        grid_spec=grid_spec,
        out_shape=jax.ShapeDtypeStruct((nb, rows, rank), F32),
        compiler_params=_params("arbitrary", "arbitrary"),
        name="attn_decode",
    )(page_table, q_full, new_keys, cache_ckv, cache_krope_t)


def _oup_kernel(o_ref, w_ref, prev_ref, out_ref):
    del prev_ref
    out_ref[...] = jnp.dot(o_ref[0], w_ref[...].astype(BF16),
                           preferred_element_type=F32).astype(out_ref.dtype)


def o_up(o_lat, w_uv2, vdim, o_buf, row_start):
    heads, m, rank = o_lat.shape
    assert row_start % m == 0
    first = row_start // m
    return pl.pallas_call(
        _oup_kernel,
        grid=(heads,),
        in_specs=[pl.BlockSpec((1, m, rank), lambda h: (h, 0, 0)),
                  pl.BlockSpec((rank, vdim), lambda h: (0, h)),
                  pl.BlockSpec(memory_space=pl.ANY)],
        out_specs=pl.BlockSpec((m, vdim), lambda h: (first, h)),
        out_shape=jax.ShapeDtypeStruct(o_buf.shape, o_buf.dtype),
        input_output_aliases={2: 0},
        compiler_params=_params("arbitrary"),
        name="o_up",
    )(o_lat, w_uv2, o_buf)


def _rope_tables(pos, rope_dim, nope_pad, width):
    half = rope_dim // 2
    freqs = ROPE_THETA ** (-jnp.arange(half, dtype=F32) / half)
    ang = pos.astype(F32)[:, None] * freqs[None, :]
    cos, sin = jnp.cos(ang), jnp.sin(ang)
    n = pos.shape[0]
    zeros = lambda w: jnp.zeros((n, w), F32)
    tail = width - nope_pad - rope_dim
    ta = jnp.concatenate([jnp.ones((n, nope_pad), F32), cos, cos, zeros(tail)], axis=1)
    tb = jnp.concatenate([zeros(nope_pad + half), sin, zeros(tail)], axis=1)
    tc = jnp.concatenate([zeros(nope_pad), -sin, zeros(half + tail)], axis=1)
    return ta, tb, tc


def kernel(x_prompt, x_sample, cache_ckv, cache_krope, page_table, state_C, state_n, state_m,
           g_mix, g_ffn, w_ffn_in, w_ffn_out, w_a_in, b_a_gates, g_a_head, w_a_out,
           g_kv_in, w_kv_down, g_kv_latent, w_uk, w_uv, w_q_down, g_q_latent, w_q_up,
           w_b_out, g_final):
    bp, tp, d = x_prompt.shape
    bs, ts, _ = x_sample.shape
    mp, ms = bp * tp, bs * ts
    n_a = state_C.shape[0]
    n_layers = g_mix.shape[0]
    assert n_a == 1 and n_layers == 2, "one mLSTM layer followed by one MLA layer"
    _, _, heads_a, dk, dv = state_C.shape
    hk, hv = heads_a * dk, heads_a * dv
    rank, heads_b, nope = w_uk.shape
    vdim = w_uv.shape[2]
    rope = cache_krope.shape[2]
    psz = cache_ckv.shape[1]
    past = page_table.shape[1] * psz
    hw = nope + 2 * rope
    assert nope % LANES_V7X == 0 and hw % LANES_V7X == 0 and (2 * rope) % LANES_V7X == 0
    attn_scale = (nope + rope) ** -0.5

    x = jnp.concatenate([x_prompt.reshape(mp, d), x_sample.reshape(ms, d)], axis=0)
    pos = jnp.concatenate([jnp.tile(jnp.arange(tp), bp), jnp.tile(past + jnp.arange(ts), bs)])
    tabs = _rope_tables(pos, rope, 0, 2 * rope)

    (xn,) = rmsnorm_rows(x, g_mix[0:1])
    w_in_t = jnp.swapaxes(w_a_in[0], 0, 1)
    proj = matmul(xn, w_in_t, n_out=2 * hk + 2 * hv, out_dtype=BF16, w_t=True, tn_max=1024,
                  name="mlstm_in_proj")
    gates = gates_t(xn, w_in_t[2 * hk + 2 * hv:])
    bias = b_a_gates[0].reshape(2 * heads_a, 1)
    g_head = g_a_head[0].reshape(1, hv)

    hc = jnp.zeros((mp + ms, hv), BF16)
    hc, c_p, n_p, m_p = mlstm_prompt(proj, gates, bias, g_head, hc, bp, tp, heads_a, dk, dv,
                                     chunk=min(512, tp))
    group = 8
    rows = group * ts
    gates_blk = gates[:, mp:].reshape(2 * heads_a, bs // group, rows).transpose(1, 0, 2)
    m0_blk = jnp.repeat(state_m[0], ts, axis=0).reshape(bs // group, rows, heads_a)
    hc, c_s, n_s, m_s = mlstm_sample(proj, gates_blk, bias, g_head, m0_blk, state_n[0],
                                     state_C[0], hc, mp, heads_a, dk, dv, ts, group)
    x, xn = proj_res_norm(hc, w_a_out[0], x, g_ffn[0:1], name="mlstm_out_proj")

    x = matmul(swiglu_in(xn, w_ffn_in, 0), w_ffn_out, layer=0, res=x, w_buffers=1, name="ffn_out")

    w_kv_pad = jnp.pad(w_kv_down, ((0, 0), (0, rope)))
    g_lat = g_kv_latent.reshape(1, rank)
    g_kv = g_kv_in.reshape(1, d)
    ckv_p, ckvb_p, kr_p, krb_p = kv_down(x, g_kv, w_kv_pad, g_lat, tabs, rank, rope, 0, mp)
    ckv_s, ckvb_s, kr_s, krb_s = kv_down(x, g_kv, w_kv_pad, g_lat, tabs, rank, rope, mp, ms)

    xq = q_down(x, g_mix[1:2], w_q_down[0], g_q_latent[0].reshape(1, rank))
    w_qup_pad = jnp.pad(w_q_up[0].reshape(rank, heads_b, nope + rope),
                        ((0, 0), (0, 0), (0, rope))).reshape(rank, heads_b * hw)
    q = q_up(xq, w_qup_pad, tabs, hw, hw - nope)

    w_uk2 = w_uk.reshape(rank, heads_b * nope)
    w_uv2 = w_uv.reshape(rank, heads_b * vdim)
    kv = matmul(ckvb_p, jnp.concatenate([w_uk2, w_uv2], axis=1), out_dtype=BF16, tn_max=1024,
                name="kv_up")
    tq = min(2048, tp)
    o = jnp.zeros((mp + ms, heads_b * vdim), BF16)
    o = attn_prompt(q, kv, krb_p, o, bp, tp, heads_b, nope, vdim, attn_scale,
                    tq=tq, parts=max(1, tq // 256))

    q_lat = q_latent(q, w_uk2, heads_b, nope, mp, ms)
    width = rank + 2 * rope
    q_full = q_lat.reshape(heads_b, bs, ts, width).transpose(1, 2, 0, 3).reshape(bs, ts * heads_b, width)
    new_keys = jnp.concatenate([ckvb_s, krb_s], axis=1).reshape(bs, ts, width)
    new_keys = jnp.pad(new_keys, ((0, 0), (0, BF16_SUBLANES_V7X - ts), (0, 0)))
    cache_krope_t = jnp.swapaxes(cache_krope, 1, 2)
    o_lat = attn_decode(q_full, new_keys, cache_ckv, cache_krope_t, page_table, attn_scale, ts,
                        pages=min(64, page_table.shape[1]))
    o_lat = o_lat.reshape(bs, ts, heads_b, rank).transpose(2, 0, 1, 3).reshape(heads_b, ms, rank)
    o = o_up(o_lat.astype(BF16), w_uv2, vdim, o, mp)

    x, xn = proj_res_norm(o, w_b_out[0], x, g_ffn[1:2], name="mla_out_proj")

    x = matmul(swiglu_in(xn, w_ffn_in, 1), w_ffn_out, layer=1, res=x, w_buffers=1, name="ffn_out")
    g_fin = g_final.reshape(1, d)
    (y_p,) = rmsnorm_rows(x, g_fin, out_dtype=F32, row_start=0, n_rows=mp)
    (y_s,) = rmsnorm_rows(x, g_fin, out_dtype=F32, row_start=mp, n_rows=ms)

    return (y_p.reshape(bp, tp, d), y_s.reshape(bs, ts, d),
            c_p[None], n_p[None], m_p[None, :, :, 0],
            ckv_p.reshape(bp, tp, rank), kr_p.reshape(bp, tp, rope),
            c_s[None], n_s[None], m_s[None, :, :, 0],
            ckv_s.reshape(bs, ts, rank), kr_s.reshape(bs, ts, rope))
```

```python
import functools

import jax
import jax.numpy as jnp
from jax import lax
from jax.experimental import pallas as pl
from jax.experimental.pallas import tpu as pltpu

F32 = jnp.float32
BF16 = jnp.bfloat16

EPS = 1e-6
GATE_CAP = 15.0
ROPE_THETA = 10000.0
LOG2E = 1.4426950408889634

LANES_V7X = 128
BF16_SUBLANES_V7X = 16
VMEM_BYTES_V7X = 64 * 1024 * 1024
VMEM_BUDGET = VMEM_BYTES_V7X * 5 // 8
VMEM_LIMIT = VMEM_BYTES_V7X * 7 // 8

NT = (((1,), (1,)), ((), ()))
TN = (((0,), (0,)), ((), ()))


def _params(*sem):
    return pltpu.CompilerParams(dimension_semantics=sem, vmem_limit_bytes=VMEM_LIMIT)


def _largest_divisor(n, cap, mult):
    best = None
    for d in range(mult, min(n, cap) + 1, mult):
        if n % d == 0:
            best = d
    return n if best is None else best


def _row_tile(m, cap=1088):
    return _largest_divisor(m, cap, BF16_SUBLANES_V7X)


def _rms_kernel(x_ref, g_ref, *o_refs):
    x = x_ref[...]
    y = x * lax.rsqrt(jnp.mean(x * x, axis=-1, keepdims=True) + EPS)
    for i, o_ref in enumerate(o_refs):
        o_ref[...] = (y * g_ref[i:i + 1, :]).astype(o_ref.dtype)


def rmsnorm_rows(x, gains, out_dtype=BF16, row_start=0, n_rows=None):
    d = x.shape[1]
    m = x.shape[0] if n_rows is None else n_rows
    n = gains.shape[0]
    tm = _largest_divisor(m, 512, BF16_SUBLANES_V7X)
    assert row_start % tm == 0
    first = row_start // tm
    outs = pl.pallas_call(
        _rms_kernel,
        grid=(m // tm,),
        in_specs=[pl.BlockSpec((tm, d), lambda i: (i + first, 0)),
                  pl.BlockSpec((n, d), lambda i: (0, 0))],
        out_specs=[pl.BlockSpec((tm, d), lambda i: (i, 0))] * n,
        out_shape=[jax.ShapeDtypeStruct((m, d), out_dtype)] * n,
        compiler_params=_params("arbitrary"),
        name="rmsnorm_rows",
    )(x, gains)
    return outs


def _mm_tiles(m, k, n, out_bytes, n_w=1, has_res=False, w_buffers=2, tn_max=512):
    best = None
    for tn in (1024, 512, 256, 128):
        if n % tn or tn > tn_max:
            continue
        for tm in range(BF16_SUBLANES_V7X, min(m, 1088) + 1, BF16_SUBLANES_V7X):
            if m % tm:
                continue
            need = (2 * tm * k * 2 + n_w * (w_buffers * k * tn * 4 + k * tn * 2)
                    + 2 * tm * tn * out_bytes + (2 * tm * tn * 4 if has_res else 0))
            if need <= VMEM_BUDGET and (best is None or tm * tn > best[0] * best[1]):
                best = (tm, tn)
    assert best is not None, (m, k, n)
    return best


def _mm_kernel(a_ref, w_ref, *rest, has_res, w_t):
    if has_res:
        r_ref, o_ref, w_sc = rest
    else:
        o_ref, w_sc = rest

    @pl.when(pl.program_id(1) == 0)
    def _():
        w_sc[...] = w_ref[...].astype(BF16)

    if w_t:
        acc = lax.dot_general(a_ref[...], w_sc[...], NT, preferred_element_type=F32)
    else:
        acc = jnp.dot(a_ref[...], w_sc[...], preferred_element_type=F32)
    if has_res:
        acc = acc + r_ref[...]
    o_ref[...] = acc.astype(o_ref.dtype)


def _layer_spec(block, index_map, layer, buffers=2):
    mode = {} if buffers == 2 else {"pipeline_mode": pl.Buffered(buffers)}
    if layer is None:
        return pl.BlockSpec(block, index_map, **mode)
    return pl.BlockSpec((None,) + block, lambda *g: (layer,) + index_map(*g), **mode)


def matmul(a, w, n_out=None, res=None, out_dtype=F32, w_t=False, layer=None, w_buffers=2,
           tn_max=512, name="matmul"):
    m, k = a.shape
    w_shape = w.shape if layer is None else w.shape[1:]
    n = (w_shape[0] if w_t else w_shape[1]) if n_out is None else n_out
    tm, tn = _mm_tiles(m, k, n, jnp.dtype(out_dtype).itemsize, has_res=res is not None,
                       w_buffers=w_buffers, tn_max=tn_max)
    w_block = (tn, k) if w_t else (k, tn)
    w_map = (lambda j, i: (j, 0)) if w_t else (lambda j, i: (0, j))
    in_specs = [pl.BlockSpec((tm, k), lambda j, i: (i, 0)),
                _layer_spec(w_block, w_map, layer, w_buffers)]
    args = [a, w]
    if res is not None:
        in_specs.append(pl.BlockSpec((tm, tn), lambda j, i: (i, j)))
        args.append(res)
    return pl.pallas_call(
        functools.partial(_mm_kernel, has_res=res is not None, w_t=w_t),
        grid=(n // tn, m // tm),
        in_specs=in_specs,
        out_specs=pl.BlockSpec((tm, tn), lambda j, i: (i, j)),
        out_shape=jax.ShapeDtypeStruct((m, n), out_dtype),
        scratch_shapes=[pltpu.VMEM(w_block, BF16)],
        compiler_params=_params("arbitrary", "arbitrary"),
        name=name,
    )(*args)


def _proj_norm_kernel(a_ref, w_ref, r_ref, g_ref, x_ref, *rest):
    xn_refs, w_sc = rest[:-1], rest[-1]

    @pl.when(pl.program_id(0) == 0)
    def _():
        w_sc[...] = w_ref[...].astype(BF16)

    x = r_ref[...] + jnp.dot(a_ref[...], w_sc[...], preferred_element_type=F32)
    x_ref[...] = x
    y = x * lax.rsqrt(jnp.mean(x * x, axis=-1, keepdims=True) + EPS)
    for i, o_ref in enumerate(xn_refs):
        o_ref[...] = (y * g_ref[i:i + 1, :]).astype(o_ref.dtype)


def proj_res_norm(a, w, res, gains, name):
    m, k = a.shape
    n = w.shape[1]
    ng = gains.shape[0]
    tm = _row_tile(m, 272)
    row = lambda i: (i, 0)
    outs = pl.pallas_call(
        _proj_norm_kernel,
        grid=(m // tm,),
        in_specs=[pl.BlockSpec((tm, k), row),
                  pl.BlockSpec((k, n), lambda i: (0, 0), pipeline_mode=pl.Buffered(1)),
                  pl.BlockSpec((tm, n), row),
                  pl.BlockSpec((ng, n), lambda i: (0, 0))],
        out_specs=[pl.BlockSpec((tm, n), row)] * (1 + ng),
        out_shape=[jax.ShapeDtypeStruct((m, n), F32)] + [jax.ShapeDtypeStruct((m, n), BF16)] * ng,
        scratch_shapes=[pltpu.VMEM((k, n), BF16)],
        compiler_params=_params("arbitrary"),
        name=name,
    )(a, w, res, gains)
    return outs


def _swiglu_kernel(a_ref, wg_ref, wu_ref, o_ref, wg_sc, wu_sc):
    @pl.when(pl.program_id(1) == 0)
    def _():
        wg_sc[...] = wg_ref[...].astype(BF16)
        wu_sc[...] = wu_ref[...].astype(BF16)

    a = a_ref[...]
    g = jnp.dot(a, wg_sc[...], preferred_element_type=F32)
    u = jnp.dot(a, wu_sc[...], preferred_element_type=F32)
    o_ref[...] = (g * jax.nn.sigmoid(g) * u).astype(o_ref.dtype)


def swiglu_in(a, w_in, layer):
    m, k = a.shape
    f = w_in.shape[2] // 2
    tm, tf = _mm_tiles(m, k, f, 2, n_w=2)
    nf = f // tf
    return pl.pallas_call(
        _swiglu_kernel,
        grid=(nf, m // tm),
        in_specs=[pl.BlockSpec((tm, k), lambda j, i: (i, 0)),
                  _layer_spec((k, tf), lambda j, i: (0, j), layer),
                  _layer_spec((k, tf), lambda j, i: (0, j + nf), layer)],
        out_specs=pl.BlockSpec((tm, tf), lambda j, i: (i, j)),
        out_shape=jax.ShapeDtypeStruct((m, f), BF16),
        scratch_shapes=[pltpu.VMEM((k, tf), BF16), pltpu.VMEM((k, tf), BF16)],
        compiler_params=_params("arbitrary", "arbitrary"),
        name="swiglu_in",
    )(a, w_in, w_in)


def _gates_kernel(wt_ref, a_ref, o_ref):
    o_ref[...] = lax.dot_general(wt_ref[...].astype(BF16), a_ref[...], NT,
                                 preferred_element_type=F32)


def gates_t(a, w_gate_t):
    m, k = a.shape
    g = w_gate_t.shape[0]
    tm = _largest_divisor(m, 512, LANES_V7X)
    return pl.pallas_call(
        _gates_kernel,
        grid=(m // tm,),
        in_specs=[pl.BlockSpec((g, k), lambda i: (0, 0)),
                  pl.BlockSpec((tm, k), lambda i: (i, 0))],
        out_specs=pl.BlockSpec((g, tm), lambda i: (0, i)),
        out_shape=jax.ShapeDtypeStruct((g, m), F32),
        compiler_params=_params("arbitrary"),
        name="gates_t",
    )(w_gate_t, a)


def _rope_lanes(x, a, b, c):
    w = x.shape[1]
    half = w // 4
    return x * a + pltpu.roll(x, half, 1) * b + pltpu.roll(x, w - half, 1) * c


def _rmsnorm_bf16(x, g):
    return (x * lax.rsqrt(jnp.mean(x * x, axis=-1, keepdims=True) + EPS) * g).astype(BF16)


def _qdown_kernel(x_ref, gi_ref, w_ref, g_ref, o_ref, w_sc):
    @pl.when(pl.program_id(0) == 0)
    def _():
        w_sc[...] = w_ref[...].astype(BF16)

    y = jnp.dot(_rmsnorm_bf16(x_ref[...], gi_ref[...]), w_sc[...], preferred_element_type=F32)
    y = y * lax.rsqrt(jnp.mean(y * y, axis=-1, keepdims=True) + EPS)
    o_ref[...] = (y * g_ref[...]).astype(o_ref.dtype)


def q_down(x, g_in, w, g):
    m, k = x.shape
    r = w.shape[1]
    tm = _row_tile(m)
    return pl.pallas_call(
        _qdown_kernel,
        grid=(m // tm,),
        in_specs=[pl.BlockSpec((tm, k), lambda i: (i, 0)),
                  pl.BlockSpec((1, k), lambda i: (0, 0)),
                  pl.BlockSpec((k, r), lambda i: (0, 0)),
                  pl.BlockSpec((1, r), lambda i: (0, 0))],
        out_specs=pl.BlockSpec((tm, r), lambda i: (i, 0)),
        out_shape=jax.ShapeDtypeStruct((m, r), BF16),
        scratch_shapes=[pltpu.VMEM((k, r), BF16)],
        compiler_params=_params("arbitrary"),
        name="q_down",
    )(x, g_in, w, g)


def _qup_kernel(a_ref, w_ref, ta_ref, tb_ref, tc_ref, o_ref, w_sc, *, hw):
    @pl.when(pl.program_id(0) == 0)
    def _():
        w_sc[...] = w_ref[...].astype(BF16)

    a = a_ref[...]
    ta, tb, tc = ta_ref[...], tb_ref[...], tc_ref[...]
    rw = ta.shape[1]
    for h in range(w_sc.shape[1] // hw):
        y = jnp.dot(a, w_sc[:, h * hw:(h + 1) * hw], preferred_element_type=F32)
        o_ref[:, h * hw:(h + 1) * hw - rw] = y[:, :hw - rw].astype(o_ref.dtype)
        o_ref[:, (h + 1) * hw - rw:(h + 1) * hw] = _rope_lanes(y[:, hw - rw:], ta, tb, tc).astype(o_ref.dtype)


def q_up(a, w_pad, tabs, hw, rw):
    m, k = a.shape
    n = w_pad.shape[1]
    tm = _row_tile(m, 544)
    assert tabs[0].shape[1] == rw
    tspec = pl.BlockSpec((tm, rw), lambda i: (i, 0))
    return pl.pallas_call(
        functools.partial(_qup_kernel, hw=hw),
        grid=(m // tm,),
        in_specs=[pl.BlockSpec((tm, k), lambda i: (i, 0)),
                  pl.BlockSpec((k, n), lambda i: (0, 0)),
                  tspec, tspec, tspec],
        out_specs=pl.BlockSpec((tm, n), lambda i: (i, 0)),
        out_shape=jax.ShapeDtypeStruct((m, n), BF16),
        scratch_shapes=[pltpu.VMEM((k, n), BF16)],
        compiler_params=_params("arbitrary"),
        name="q_up",
    )(a, w_pad, *tabs)


def _kvdown_kernel(x_ref, gi_ref, w_ref, g_ref, ta_ref, tb_ref, tc_ref,
                   ckv_ref, ckvb_ref, kr_ref, krb_ref, w_sc, *, rank):
    @pl.when(pl.program_id(0) == 0)
    def _():
        w_sc[...] = w_ref[...].astype(BF16)

    y = jnp.dot(_rmsnorm_bf16(x_ref[...], gi_ref[...]), w_sc[...], preferred_element_type=F32)
    lat = y[:, :rank]
    lat = lat * lax.rsqrt(jnp.mean(lat * lat, axis=-1, keepdims=True) + EPS) * g_ref[...]
    ckv_ref[...] = lat
    ckvb_ref[...] = lat.astype(BF16)
    kr = _rope_lanes(y[:, rank:], ta_ref[...], tb_ref[...], tc_ref[...])
    kr_ref[...] = kr[:, :kr_ref.shape[1]]
    krb_ref[...] = kr.astype(BF16)


def kv_down(x, g_in, w_pad, g, tabs, rank, rope, row_start, n_rows):
    k = x.shape[1]
    m = n_rows
    n = w_pad.shape[1]
    rw = n - rank
    tm = _row_tile(m)
    assert row_start % tm == 0
    first = row_start // tm
    assert tabs[0].shape[1] == rw
    tspec = pl.BlockSpec((tm, rw), lambda i: (i + first, 0))
    return pl.pallas_call(
        functools.partial(_kvdown_kernel, rank=rank),
        grid=(m // tm,),
        in_specs=[pl.BlockSpec((tm, k), lambda i: (i + first, 0)),
                  pl.BlockSpec((1, k), lambda i: (0, 0)),
                  pl.BlockSpec((k, n), lambda i: (0, 0)),
                  pl.BlockSpec((1, rank), lambda i: (0, 0)),
                  tspec, tspec, tspec],
        out_specs=[pl.BlockSpec((tm, rank), lambda i: (i, 0)),
                   pl.BlockSpec((tm, rank), lambda i: (i, 0)),
                   pl.BlockSpec((tm, rope), lambda i: (i, 0)),
                   pl.BlockSpec((tm, rw), lambda i: (i, 0))],
        out_shape=[jax.ShapeDtypeStruct((m, rank), F32),
                   jax.ShapeDtypeStruct((m, rank), BF16),
                   jax.ShapeDtypeStruct((m, rope), F32),
                   jax.ShapeDtypeStruct((m, rw), BF16)],
        scratch_shapes=[pltpu.VMEM((k, n), BF16)],
        compiler_params=_params("arbitrary"),
        name="kv_down",
    )(x, g_in, w_pad, g, *tabs)


def _split3(x):
    hi = x.astype(BF16)
    r = x - hi.astype(F32)
    mid = r.astype(BF16)
    lo = (r - mid.astype(F32)).astype(BF16)
    return hi, mid, lo


def _dot_exact(x, m01):
    return sum(jnp.dot(p, m01, preferred_element_type=F32) for p in _split3(x))


def _transpose_exact(eye, x):
    return sum(lax.dot_general(eye, p, NT, preferred_element_type=F32) for p in _split3(x))


def _log_sigmoid(x):
    return jnp.minimum(x, 0.0) - jnp.log1p(jnp.exp(-jnp.abs(x)))


def _softcap(z):
    return GATE_CAP * jnp.tanh(z / GATE_CAP)


def _mlstm_prompt_kernel(q_ref, k_ref, v_ref, o_ref, gt_ref, bias_ref, gh_ref, hc_prev_ref,
                         hc_ref, c_out, n_out, m_out, c_sc, n_sc, m_sc, *, heads, dk, dv):
    del hc_prev_ref
    c = pl.program_id(1)
    L = q_ref.shape[0]
    scale = dk ** -0.5

    @pl.when(c == 0)
    def _():
        c_sc[...] = jnp.zeros_like(c_sc)
        n_sc[...] = jnp.zeros_like(n_sc)
        m_sc[...] = jnp.zeros_like(m_sc)

    z = _softcap(gt_ref[...] + bias_ref[...])
    li_r = z[:heads]
    lf_r = _log_sigmoid(z[heads:])
    row = lax.broadcasted_iota(jnp.int32, (L, L), 0)
    col = lax.broadcasted_iota(jnp.int32, (L, L), 1)
    causal = col <= row
    upper = (row <= col).astype(BF16)
    eye = (row == col).astype(BF16)
    b_r = _dot_exact(lf_r, upper)
    cols = _transpose_exact(eye, jnp.concatenate([li_r, b_r], axis=0))

    for h in range(heads):
        q = q_ref[:, h * dk:(h + 1) * dk]
        k = k_ref[:, h * dk:(h + 1) * dk]
        v = v_ref[:, h * dv:(h + 1) * dv]
        li_c = cols[:, h:h + 1]
        b_c = cols[:, heads + h:heads + h + 1]
        m_prev = m_sc[h:h + 1, 0:1]
        n_prev = n_sc[h:h + 1, :]
        c_prev = c_sc[h]

        logw = jnp.where(causal, b_c - b_r[h:h + 1, :] + li_r[h:h + 1, :], -jnp.inf)
        log_inter = b_c + m_prev
        m_t = jnp.maximum(log_inter, jnp.max(logw, axis=1, keepdims=True))
        s = lax.dot_general(q, k, NT, preferred_element_type=F32) * scale * jnp.exp(logw - m_t)
        w_inter = jnp.exp(log_inter - m_t)
        num = (jnp.dot(s.astype(BF16), v, preferred_element_type=F32)
               + w_inter * jnp.dot(q, c_prev.astype(BF16), preferred_element_type=F32))
        qn = jnp.sum(q.astype(F32) * n_prev, axis=1, keepdims=True)
        den = jnp.sum(s, axis=1, keepdims=True) + w_inter * qn
        hh = num / jnp.maximum(jnp.abs(den), jnp.exp(-m_t))
        hh = hh * lax.rsqrt(jnp.mean(hh * hh, axis=1, keepdims=True) + EPS)
        og = o_ref[:, h * dv:(h + 1) * dv].astype(F32)
        hc_ref[:, h * dv:(h + 1) * dv] = (
            hh * gh_ref[:, h * dv:(h + 1) * dv] * jax.nn.sigmoid(og)).astype(hc_ref.dtype)

        m_end = m_t[L - 1:L, :]
        b_last = b_c[L - 1:L, :]
        w_end = jnp.exp(b_last - b_c + li_c - m_end)
        decay = jnp.exp(b_last + m_prev - m_end)
        kw = k.astype(F32) * (scale * w_end)
        c_sc[h] = decay * c_prev + lax.dot_general(kw.astype(BF16), v, TN,
                                                   preferred_element_type=F32)
        n_sc[h:h + 1, :] = decay * n_prev + jnp.sum(kw, axis=0, keepdims=True)
        m_sc[h:h + 1, :] = jnp.broadcast_to(m_end, (1, m_sc.shape[1]))

    @pl.when(c == pl.num_programs(1) - 1)
    def _():
        c_out[0] = c_sc[...]
        n_out[0] = n_sc[...]
        m_out[0] = m_sc[...]


def mlstm_prompt(proj, gates, bias, g_head, hc_buf, batch, seq, heads, dk, dv, chunk):
    hk, hv = heads * dk, heads * dv
    nc = seq // chunk
    row = lambda b, c: b * nc + c
    return pl.pallas_call(
        functools.partial(_mlstm_prompt_kernel, heads=heads, dk=dk, dv=dv),
        grid=(batch, nc),
        in_specs=[pl.BlockSpec((chunk, hk), lambda b, c: (row(b, c), 0)),
                  pl.BlockSpec((chunk, hk), lambda b, c: (row(b, c), 1)),
                  pl.BlockSpec((chunk, hv), lambda b, c: (row(b, c), 2 * hk // hv)),
                  pl.BlockSpec((chunk, hv), lambda b, c: (row(b, c), 2 * hk // hv + 1)),
                  pl.BlockSpec((2 * heads, chunk), lambda b, c: (0, row(b, c))),
                  pl.BlockSpec((2 * heads, 1), lambda b, c: (0, 0)),
                  pl.BlockSpec((1, hv), lambda b, c: (0, 0)),
                  pl.BlockSpec(memory_space=pl.ANY)],
        out_specs=[pl.BlockSpec((chunk, hv), lambda b, c: (row(b, c), 0)),
                   pl.BlockSpec((1, heads, dk, dv), lambda b, c: (b, 0, 0, 0)),
                   pl.BlockSpec((1, heads, dk), lambda b, c: (b, 0, 0)),
                   pl.BlockSpec((1, heads, LANES_V7X), lambda b, c: (b, 0, 0))],
        out_shape=[jax.ShapeDtypeStruct(hc_buf.shape, hc_buf.dtype),
                   jax.ShapeDtypeStruct((batch, heads, dk, dv), F32),
                   jax.ShapeDtypeStruct((batch, heads, dk), F32),
                   jax.ShapeDtypeStruct((batch, heads, LANES_V7X), F32)],
        scratch_shapes=[pltpu.VMEM((heads, dk, dv), F32),
                        pltpu.VMEM((heads, dk), F32),
                        pltpu.VMEM((heads, LANES_V7X), F32)],
        input_output_aliases={7: 0},
        compiler_params=_params("arbitrary", "arbitrary"),
        name="mlstm_prompt",
    )(proj, proj, proj, proj, gates, bias, g_head, hc_buf)


def _mlstm_sample_kernel(q_ref, k_ref, v_ref, o_ref, gt_ref, bias_ref, gh_ref, m0_ref,
                         n0_ref, c0_ref, hc_prev_ref, hc_ref, c_out, n_out, m_out,
                         *, heads, dk, dv, ts):
    del hc_prev_ref
    R = q_ref.shape[0]
    G = R // ts
    scale = dk ** -0.5

    z = _softcap(gt_ref[0] + bias_ref[...])
    li_r = z[:heads]
    lf_r = _log_sigmoid(z[heads:])
    row = lax.broadcasted_iota(jnp.int32, (R, R), 0)
    col = lax.broadcasted_iota(jnp.int32, (R, R), 1)
    seq_id = lambda i: sum((i >= g * ts).astype(jnp.int32) for g in range(1, G))
    same = seq_id(row) == seq_id(col)
    causal = same & (col <= row)
    upper = (same & (row <= col)).astype(BF16)
    eye = (row == col).astype(BF16)
    b_r = _dot_exact(lf_r, upper)
    cols = _transpose_exact(eye, jnp.concatenate([li_r, b_r], axis=0))
    seq_of_row = seq_id(lax.broadcasted_iota(jnp.int32, (R, 1), 0))
    m0 = m0_ref[0]

    for h in range(heads):
        q = q_ref[:, h * dk:(h + 1) * dk]
        k = k_ref[:, h * dk:(h + 1) * dk]
        v = v_ref[:, h * dv:(h + 1) * dv]
        li_c = cols[:, h:h + 1]
        b_c = cols[:, heads + h:heads + h + 1]
        m_prev = m0[:, h:h + 1]

        logw = jnp.where(causal, b_c - b_r[h:h + 1, :] + li_r[h:h + 1, :], -jnp.inf)
        log_inter = b_c + m_prev
        m_t = jnp.maximum(log_inter, jnp.max(logw, axis=1, keepdims=True))
        s = lax.dot_general(q, k, NT, preferred_element_type=F32) * scale * jnp.exp(logw - m_t)
        w_inter = jnp.exp(log_inter - m_t)

        qc = jnp.zeros((R, dv), F32)
        qn = jnp.zeros((R, 1), F32)
        qf = q.astype(F32)
        for g in range(G):
            mine = seq_of_row == g
            qc = jnp.where(mine, jnp.dot(q, c0_ref[g, h].astype(BF16),
                                         preferred_element_type=F32), qc)
            qn = jnp.where(mine, jnp.sum(qf * n0_ref[g, h:h + 1, :], axis=1, keepdims=True), qn)
        num = jnp.dot(s.astype(BF16), v, preferred_element_type=F32) + w_inter * qc
        den = jnp.sum(s, axis=1, keepdims=True) + w_inter * qn
        hh = num / jnp.maximum(jnp.abs(den), jnp.exp(-m_t))
        hh = hh * lax.rsqrt(jnp.mean(hh * hh, axis=1, keepdims=True) + EPS)
        og = o_ref[:, h * dv:(h + 1) * dv].astype(F32)
        hc_ref[:, h * dv:(h + 1) * dv] = (
            hh * gh_ref[:, h * dv:(h + 1) * dv] * jax.nn.sigmoid(og)).astype(hc_ref.dtype)

        kf = k.astype(F32)
        for g in range(G):
            last = (g + 1) * ts - 1
            m_end = m_t[last:last + 1, :]
            b_last = b_c[last:last + 1, :]
            mine = seq_of_row == g
            w_end = jnp.where(mine, jnp.exp(b_last - b_c + li_c - m_end), 0.0)
            decay = jnp.exp(b_last + m_prev[last:last + 1, :] - m_end)
            kw = kf * (scale * w_end)
            c_out[g, h] = decay * c0_ref[g, h] + lax.dot_general(
                kw.astype(BF16), v, TN, preferred_element_type=F32)
            n_out[g, h:h + 1, :] = decay * n0_ref[g, h:h + 1, :] + jnp.sum(kw, axis=0, keepdims=True)
            m_out[g, h:h + 1, :] = jnp.broadcast_to(m_end, (1, m_out.shape[2]))


def mlstm_sample(proj, gates_blk, bias, g_head, m0_blk, n0, c0, hc_buf, row_start,
                 heads, dk, dv, ts, group):
    hk, hv = heads * dk, heads * dv
    rows = group * ts
    nb = c0.shape[0]
    steps = nb // group
    assert row_start % rows == 0
    first = row_start // rows
    return pl.pallas_call(
        functools.partial(_mlstm_sample_kernel, heads=heads, dk=dk, dv=dv, ts=ts),
        grid=(steps,),
        in_specs=[pl.BlockSpec((rows, hk), lambda i: (i + first, 0)),
                  pl.BlockSpec((rows, hk), lambda i: (i + first, 1)),
                  pl.BlockSpec((rows, hv), lambda i: (i + first, 2 * hk // hv)),
                  pl.BlockSpec((rows, hv), lambda i: (i + first, 2 * hk // hv + 1)),
                  pl.BlockSpec((1, 2 * heads, rows), lambda i: (i, 0, 0)),
                  pl.BlockSpec((2 * heads, 1), lambda i: (0, 0)),
                  pl.BlockSpec((1, hv), lambda i: (0, 0)),
                  pl.BlockSpec((1, rows, heads), lambda i: (i, 0, 0)),
                  pl.BlockSpec((group, heads, dk), lambda i: (i, 0, 0)),
                  pl.BlockSpec((group, heads, dk, dv), lambda i: (i, 0, 0, 0)),
                  pl.BlockSpec(memory_space=pl.ANY)],
        out_specs=[pl.BlockSpec((rows, hv), lambda i: (i + first, 0)),
                   pl.BlockSpec((group, heads, dk, dv), lambda i: (i, 0, 0, 0)),
                   pl.BlockSpec((group, heads, dk), lambda i: (i, 0, 0)),
                   pl.BlockSpec((group, heads, LANES_V7X), lambda i: (i, 0, 0))],
        out_shape=[jax.ShapeDtypeStruct(hc_buf.shape, hc_buf.dtype),
                   jax.ShapeDtypeStruct((nb, heads, dk, dv), F32),
                   jax.ShapeDtypeStruct((nb, heads, dk), F32),
                   jax.ShapeDtypeStruct((nb, heads, LANES_V7X), F32)],
        input_output_aliases={10: 0},
        compiler_params=_params("arbitrary"),
        name="mlstm_sample",
    )(proj, proj, proj, proj, gates_blk, bias, g_head, m0_blk, n0, c0, hc_buf)


def _online_softmax(s, m_prev, c_exp):
    m_new = jnp.maximum(m_prev, jnp.max(s, axis=1, keepdims=True))
    alpha = jnp.exp2((m_prev - m_new) * c_exp)
    p = jnp.exp2((s - m_new) * c_exp)
    return m_new, alpha, p


def _attn_prompt_kernel(q_ref, kn_ref, kr_ref, v_ref, o_prev_ref, o_ref, kf_sc, vx_sc, m_sc,
                        acc_sc, *, c_exp, nope, vdim, parts):
    del o_prev_ref
    qi = pl.program_id(2)
    tq = q_ref.shape[0]
    tp = tq // parts

    @pl.when(qi == 0)
    def _():
        kf_sc[:, :nope] = kn_ref[...]
        kf_sc[:, nope:] = kr_ref[...]
        vx_sc[:, :vdim] = v_ref[...]
        lane = lax.broadcasted_iota(jnp.int32, (vx_sc.shape[0], vx_sc.shape[1] - vdim), 1)
        vx_sc[:, vdim:] = (lane == 0).astype(BF16)

    m_sc[...] = jnp.full_like(m_sc, -jnp.inf)
    acc_sc[...] = jnp.zeros_like(acc_sc)

    def update(part, k, vx, row_offset):
        s = lax.dot_general(q_ref[part * tp:(part + 1) * tp, :], k, NT, preferred_element_type=F32)
        if row_offset is not None:
            row = lax.broadcasted_iota(jnp.int32, s.shape, 0) + row_offset
            col = lax.broadcasted_iota(jnp.int32, s.shape, 1)
            s = jnp.where(col <= row, s, -jnp.inf)
        m_new, alpha, p = _online_softmax(s, m_sc[part], c_exp)
        acc_sc[part] = alpha * acc_sc[part] + jnp.dot(p.astype(BF16), vx,
                                                      preferred_element_type=F32)
        m_sc[part] = m_new

    def body(ki, carry):
        start = pl.multiple_of(ki * tq, tq)
        k = kf_sc[pl.ds(start, tq), :]
        vx = vx_sc[pl.ds(start, tq), :]
        for part in range(parts):
            update(part, k, vx, None)
        return carry

    lax.fori_loop(0, qi, body, 0)
    start = pl.multiple_of(qi * tq, tq)
    for part in range(parts):
        n_keys = (part + 1) * tp
        update(part, kf_sc[pl.ds(start, n_keys), :], vx_sc[pl.ds(start, n_keys), :], part * tp)
    for part in range(parts):
        acc = acc_sc[part]
        o_ref[part * tp:(part + 1) * tp, :] = (acc[:, :vdim] / acc[:, vdim:vdim + 1]).astype(o_ref.dtype)


def attn_prompt(q, kv, kr, o_buf, batch, seq, heads, nope, vdim, scale, tq, parts):
    hw = q.shape[1] // heads
    nq = seq // tq
    assert (heads * nope) % vdim == 0
    v_col0 = heads * nope // vdim
    return pl.pallas_call(
        functools.partial(_attn_prompt_kernel, c_exp=scale * LOG2E, nope=nope, vdim=vdim,
                          parts=parts),
        grid=(batch, heads, nq),
        in_specs=[pl.BlockSpec((tq, hw), lambda b, h, i: (b * nq + i, h)),
                  pl.BlockSpec((seq, nope), lambda b, h, i: (b, h)),
                  pl.BlockSpec((seq, hw - nope), lambda b, h, i: (b, 0)),
                  pl.BlockSpec((seq, vdim), lambda b, h, i: (b, v_col0 + h)),
                  pl.BlockSpec(memory_space=pl.ANY)],
        out_specs=pl.BlockSpec((tq, vdim), lambda b, h, i: (b * nq + i, h)),
        out_shape=jax.ShapeDtypeStruct(o_buf.shape, o_buf.dtype),
        input_output_aliases={4: 0},
        scratch_shapes=[pltpu.VMEM((seq, hw), BF16),
                        pltpu.VMEM((seq, 2 * vdim), BF16),
                        pltpu.VMEM((parts, tq // parts, 1), F32),
                        pltpu.VMEM((parts, tq // parts, 2 * vdim), F32)],
        compiler_params=_params("arbitrary", "arbitrary", "arbitrary"),
        name="attn_prompt",
    )(q, kv, kr, kv, o_buf)


def _qlat_kernel(q_ref, w_ref, o_ref, *, nope):
    q = q_ref[...]
    lat = lax.dot_general(q[:, :nope], w_ref[...].astype(BF16), NT, preferred_element_type=F32)
    o_ref[0] = jnp.concatenate([lat.astype(BF16), q[:, nope:]], axis=1)


def q_latent(q, w_uk2, heads, nope, row_start, n_rows):
    m = n_rows
    assert row_start % m == 0
    first = row_start // m
    hw = q.shape[1] // heads
    rank = w_uk2.shape[0]
    return pl.pallas_call(
        functools.partial(_qlat_kernel, nope=nope),
        grid=(heads,),
        in_specs=[pl.BlockSpec((m, hw), lambda h: (first, h)),
                  pl.BlockSpec((rank, nope), lambda h: (0, h))],
        out_specs=pl.BlockSpec((1, m, rank + hw - nope), lambda h: (h, 0, 0)),
        out_shape=jax.ShapeDtypeStruct((heads, m, rank + hw - nope), BF16),
        compiler_params=_params("arbitrary"),
        name="q_latent",
    )(q, w_uk2)


def _decode_kernel(pt_ref, q_ref, new_ref, ck_hbm, kr_hbm, o_ref, ck_buf, kr_buf, sems,
                   kc_sc, krt_sc, m_sc, l_sc, acc_sc, *, pages, c_exp, rank, rope, ts):
    b, j = pl.program_id(0), pl.program_id(1)
    nb, nj = pl.num_programs(0), pl.num_programs(1)
    psz = ck_buf.shape[2]
    step = b * nj + j
    slot = lax.rem(step, 2)

    def page_copies(bb, jj, sl):
        copies = []
        for p in range(pages):
            page = pt_ref[bb, jj * pages + p]
            copies.append(pltpu.make_async_copy(ck_hbm.at[page], ck_buf.at[sl, p], sems.at[0, sl]))
            copies.append(pltpu.make_async_copy(kr_hbm.at[page], kr_buf.at[sl, p], sems.at[1, sl]))
        return copies

    def start_all(copies):
        for c in copies:
            c.start()

    @pl.when(step == 0)
    def _():
        start_all(page_copies(b, j, slot))

    @pl.when(step + 1 < nb * nj)
    def _():
        last = j == nj - 1
        start_all(page_copies(jnp.where(last, b + 1, b), jnp.where(last, 0, j + 1), 1 - slot))

    for c in page_copies(b, j, slot):
        c.wait()

    @pl.when(j == 0)
    def _():
        m_sc[...] = jnp.full_like(m_sc, -jnp.inf)
        l_sc[...] = jnp.zeros_like(l_sc)
        acc_sc[...] = jnp.zeros_like(acc_sc)

    for p in range(pages):
        kc_sc[p * psz:(p + 1) * psz, :] = ck_buf[slot, p].astype(BF16)
        krt_sc[:, p * psz:(p + 1) * psz] = kr_buf[slot, p].astype(BF16)

    q = q_ref[0]
    s = (lax.dot_general(q[:, :rank], kc_sc[...], NT, preferred_element_type=F32)
         + jnp.dot(q[:, rank:rank + rope], krt_sc[...], preferred_element_type=F32))
    m, alpha, p = _online_softmax(s, m_sc[...], c_exp)
    l = alpha * l_sc[...] + jnp.sum(p, axis=1, keepdims=True)
    acc = alpha * acc_sc[...] + jnp.dot(p.astype(BF16), kc_sc[...], preferred_element_type=F32)
    m_sc[...] = m
    l_sc[...] = l
    acc_sc[...] = acc

    @pl.when(j == pl.num_programs(1) - 1)
    def _():
        new = new_ref[0]
        s2 = lax.dot_general(q, new, NT, preferred_element_type=F32)
        qrow = lax.broadcasted_iota(jnp.int32, s2.shape, 0)
        heads = q.shape[0] // ts
        tok = sum((qrow >= t * heads).astype(jnp.int32) for t in range(1, ts))
        key = lax.broadcasted_iota(jnp.int32, s2.shape, 1)
        s2 = jnp.where(key <= tok, s2, -jnp.inf)
        _, alpha, p = _online_softmax(s2, m, c_exp)
        l2 = alpha * l + jnp.sum(p, axis=1, keepdims=True)
        acc2 = alpha * acc + jnp.dot(p.astype(BF16), new[:, :rank], preferred_element_type=F32)
        o_ref[0] = acc2 / l2


def attn_decode(q_full, new_keys, cache_ckv, cache_krope_t, page_table, scale, ts, pages):
    nb, rows, width = q_full.shape
    _, psz, rank = cache_ckv.shape
    rope = cache_krope_t.shape[1]
    n_pages = page_table.shape[1]
    nj = n_pages // pages

    grid_spec = pltpu.PrefetchScalarGridSpec(
        num_scalar_prefetch=1,
        grid=(nb, nj),
        in_specs=[pl.BlockSpec((1, rows, width), lambda b, j, pt: (b, 0, 0)),
                  pl.BlockSpec((1, new_keys.shape[1], width), lambda b, j, pt: (b, 0, 0)),
                  pl.BlockSpec(memory_space=pl.ANY),
                  pl.BlockSpec(memory_space=pl.ANY)],
        out_specs=pl.BlockSpec((1, rows, rank), lambda b, j, pt: (b, 0, 0)),
        scratch_shapes=[pltpu.VMEM((2, pages, psz, rank), cache_ckv.dtype),
                        pltpu.VMEM((2, pages, rope, psz), cache_krope_t.dtype),
                        pltpu.SemaphoreType.DMA((2, 2)),
                        pltpu.VMEM((pages * psz, rank), BF16),
                        pltpu.VMEM((rope, pages * psz), BF16),
                        pltpu.VMEM((rows, 1), F32),
                        pltpu.VMEM((rows, 1), F32),
                        pltpu.VMEM((rows, rank), F32)],
    )
    return pl.pallas_call(
        functools.partial(_decode_kernel, pages=pages, c_exp=scale * LOG2E, rank=rank,
                          rope=rope, ts=ts),
        grid_spec=grid_spec,
        out_shape=jax.ShapeDtypeStruct((nb, rows, rank), F32),
        compiler_params=_params("arbitrary", "arbitrary"),
        name="attn_decode",
    )(page_table, q_full, new_keys, cache_ckv, cache_krope_t)


def _oup_kernel(o_ref, w_ref, prev_ref, out_ref):
    del prev_ref
    out_ref[...] = jnp.dot(o_ref[0], w_ref[...].astype(BF16),
                           preferred_element_type=F32).astype(out_ref.dtype)


def o_up(o_lat, w_uv2, vdim, o_buf, row_start):
    heads, m, rank = o_lat.shape
    assert row_start % m == 0
    first = row_start // m
    return pl.pallas_call(
        _oup_kernel,
        grid=(heads,),
        in_specs=[pl.BlockSpec((1, m, rank), lambda h: (h, 0, 0)),
                  pl.BlockSpec((rank, vdim), lambda h: (0, h)),
                  pl.BlockSpec(memory_space=pl.ANY)],
        out_specs=pl.BlockSpec((m, vdim), lambda h: (first, h)),
        out_shape=jax.ShapeDtypeStruct(o_buf.shape, o_buf.dtype),
        input_output_aliases={2: 0},
        compiler_params=_params("arbitrary"),
        name="o_up",
    )(o_lat, w_uv2, o_buf)


def _rope_tables(pos, rope_dim, nope_pad, width):
    half = rope_dim // 2
    freqs = ROPE_THETA ** (-jnp.arange(half, dtype=F32) / half)
    ang = pos.astype(F32)[:, None] * freqs[None, :]
    cos, sin = jnp.cos(ang), jnp.sin(ang)
    n = pos.shape[0]
    zeros = lambda w: jnp.zeros((n, w), F32)
    tail = width - nope_pad - rope_dim
    ta = jnp.concatenate([jnp.ones((n, nope_pad), F32), cos, cos, zeros(tail)], axis=1)
    tb = jnp.concatenate([zeros(nope_pad + half), sin, zeros(tail)], axis=1)
    tc = jnp.concatenate([zeros(nope_pad), -sin, zeros(half + tail)], axis=1)
    return ta, tb, tc


def kernel(x_prompt, x_sample, cache_ckv, cache_krope, page_table, state_C, state_n, state_m,
           g_mix, g_ffn, w_ffn_in, w_ffn_out, w_a_in, b_a_gates, g_a_head, w_a_out,
           g_kv_in, w_kv_down, g_kv_latent, w_uk, w_uv, w_q_down, g_q_latent, w_q_up,
           w_b_out, g_final):
    bp, tp, d = x_prompt.shape
    bs, ts, _ = x_sample.shape
    mp, ms = bp * tp, bs * ts
    n_a = state_C.shape[0]
    n_layers = g_mix.shape[0]
    assert n_a == 1 and n_layers == 2, "one mLSTM layer followed by one MLA layer"
    _, _, heads_a, dk, dv = state_C.shape
    hk, hv = heads_a * dk, heads_a * dv
    rank, heads_b, nope = w_uk.shape
    vdim = w_uv.shape[2]
    rope = cache_krope.shape[2]
    psz = cache_ckv.shape[1]
    past = page_table.shape[1] * psz
    hw = nope + 2 * rope
    assert nope % LANES_V7X == 0 and hw % LANES_V7X == 0 and (2 * rope) % LANES_V7X == 0
    attn_scale = (nope + rope) ** -0.5

    x = jnp.concatenate([x_prompt.reshape(mp, d), x_sample.reshape(ms, d)], axis=0)
    pos = jnp.concatenate([jnp.tile(jnp.arange(tp), bp), jnp.tile(past + jnp.arange(ts), bs)])
    tabs = _rope_tables(pos, rope, 0, 2 * rope)

    (xn,) = rmsnorm_rows(x, g_mix[0:1])
    w_in_t = jnp.swapaxes(w_a_in[0], 0, 1)
    proj = matmul(xn, w_in_t, n_out=2 * hk + 2 * hv, out_dtype=BF16, w_t=True, tn_max=1024,
                  name="mlstm_in_proj")
    gates = gates_t(xn, w_in_t[2 * hk + 2 * hv:])
    bias = b_a_gates[0].reshape(2 * heads_a, 1)
    g_head = g_a_head[0].reshape(1, hv)

    hc = jnp.zeros((mp + ms, hv), BF16)
    hc, c_p, n_p, m_p = mlstm_prompt(proj, gates, bias, g_head, hc, bp, tp, heads_a, dk, dv,
                                     chunk=min(512, tp))
    group = 8
    rows = group * ts
    gates_blk = gates[:, mp:].reshape(2 * heads_a, bs // group, rows).transpose(1, 0, 2)
    m0_blk = jnp.repeat(state_m[0], ts, axis=0).reshape(bs // group, rows, heads_a)
    hc, c_s, n_s, m_s = mlstm_sample(proj, gates_blk, bias, g_head, m0_blk, state_n[0],
                                     state_C[0], hc, mp, heads_a, dk, dv, ts, group)
    x, xn = proj_res_norm(hc, w_a_out[0], x, g_ffn[0:1], name="mlstm_out_proj")

    x = matmul(swiglu_in(xn, w_ffn_in, 0), w_ffn_out, layer=0, res=x, w_buffers=1, name="ffn_out")

    w_kv_pad = jnp.pad(w_kv_down, ((0, 0), (0, rope)))
    g_lat = g_kv_latent.reshape(1, rank)
    g_kv = g_kv_in.reshape(1, d)
    ckv_p, ckvb_p, kr_p, krb_p = kv_down(x, g_kv, w_kv_pad, g_lat, tabs, rank, rope, 0, mp)
    ckv_s, ckvb_s, kr_s, krb_s = kv_down(x, g_kv, w_kv_pad, g_lat, tabs, rank, rope, mp, ms)

    xq = q_down(x, g_mix[1:2], w_q_down[0], g_q_latent[0].reshape(1, rank))
    w_qup_pad = jnp.pad(w_q_up[0].reshape(rank, heads_b, nope + rope),
                        ((0, 0), (0, 0), (0, rope))).reshape(rank, heads_b * hw)
    q = q_up(xq, w_qup_pad, tabs, hw, hw - nope)

    w_uk2 = w_uk.reshape(rank, heads_b * nope)
    w_uv2 = w_uv.reshape(rank, heads_b * vdim)
    kv = matmul(ckvb_p, jnp.concatenate([w_uk2, w_uv2], axis=1), out_dtype=BF16, tn_max=1024,
                name="kv_up")
    tq = min(2048, tp)
    o = jnp.zeros((mp + ms, heads_b * vdim), BF16)
    o = attn_prompt(q, kv, krb_p, o, bp, tp, heads_b, nope, vdim, attn_scale,
                    tq=tq, parts=max(1, tq // 256))

    q_lat = q_latent(q, w_uk2, heads_b, nope, mp, ms)
    width = rank + 2 * rope
    q_full = q_lat.reshape(heads_b, bs, ts, width).transpose(1, 2, 0, 3).reshape(bs, ts * heads_b, width)
    new_keys = jnp.concatenate([ckvb_s, krb_s], axis=1).reshape(bs, ts, width)
    new_keys = jnp.pad(new_keys, ((0, 0), (0, BF16_SUBLANES_V7X - ts), (0, 0)))
    cache_krope_t = jnp.swapaxes(cache_krope, 1, 2)
    o_lat = attn_decode(q_full, new_keys, cache_ckv, cache_krope_t, page_table, attn_scale, ts,
                        pages=min(64, page_table.shape[1]))
    o_lat = o_lat.reshape(bs, ts, heads_b, rank).transpose(2, 0, 1, 3).reshape(heads_b, ms, rank)
    o = o_up(o_lat.astype(BF16), w_uv2, vdim, o, mp)

    x, xn = proj_res_norm(o, w_b_out[0], x, g_ffn[1:2], name="mla_out_proj")

    x = matmul(swiglu_in(xn, w_ffn_in, 1), w_ffn_out, layer=1, res=x, w_buffers=1, name="ffn_out")
    g_fin = g_final.reshape(1, d)
    (y_p,) = rmsnorm_rows(x, g_fin, out_dtype=F32, row_start=0, n_rows=mp)
    (y_s,) = rmsnorm_rows(x, g_fin, out_dtype=F32, row_start=mp, n_rows=ms)

    return (y_p.reshape(bp, tp, d), y_s.reshape(bs, ts, d),
            c_p[None], n_p[None], m_p[None, :, :, 0],
            ckv_p.reshape(bp, tp, rank), kr_p.reshape(bp, tp, rope),
            c_s[None], n_s[None], m_s[None, :, :, 0],
            ckv_s.reshape(bs, ts, rank), kr_s.reshape(bs, ts, rope))
```
